```python
import math
import jax, jax.numpy as jnp
from jax import lax
import numpy as np

D_MODEL = 1024
BATCH = 8
SEQ = 8192
DEPTH = 4

D_MIX = D_MODEL
A_GROUPS = 8
A_GROUP_DIM = 64
A_WIDTH = A_GROUPS * A_GROUP_DIM
A_CONV = 3
DN_HEADS = 4
DN_HEAD_DIM = 128
DN_WIDTH = DN_HEADS * DN_HEAD_DIM
DN_CONV = 4
CHUNK = 64
D_FF = 2816
FF_CONV = 3
P_IN = 3 * A_WIDTH + 4 * DN_WIDTH + 2 * DN_HEADS
N_MOD = 6
EPS = 1e-6

kernel_name = "hybrid_shortconv_gdn_convffn_adaln"


def rmsnorm(x, w):
    xf = x.astype(jnp.float32)
    y = xf * lax.rsqrt(jnp.mean(xf * xf, axis=-1, keepdims=True) + EPS)
    return (y * w.astype(jnp.float32)).astype(x.dtype)


def l2norm(x):
    xf = x.astype(jnp.float32)
    return xf * lax.rsqrt(jnp.sum(xf * xf, axis=-1, keepdims=True) + EPS)


def causal_dwconv(x, w):
    width, ch = w.shape
    return lax.conv_general_dilated(
        x, w[:, None, :].astype(x.dtype), window_strides=(1,), padding=[(width - 1, 0)],
        dimension_numbers=('NWC', 'WIO', 'NWC'), feature_group_count=ch)


def to_chunks(t):
    b, l, h = t.shape[:3]
    t = t.reshape(b, l // CHUNK, CHUNK, h, *t.shape[3:])
    return jnp.moveaxis(t, 3, 1)


def gated_delta_rule(q, k, v, g, beta):
    b, l, h, dk = q.shape
    dv = v.shape[-1]
    q, k, v, g, beta = (to_chunks(t) for t in (q * dk ** -0.5, k, v, g, beta))
    g = jnp.cumsum(g, axis=-1)
    pos = jnp.arange(CHUNK)
    causal = pos[:, None] >= pos[None, :]
    strict = pos[:, None] > pos[None, :]
    gdiff = jnp.where(causal, g[..., :, None] - g[..., None, :], 0.0)
    decay = jnp.where(causal, jnp.exp(gdiff), 0.0)
    k_beta = k * beta[..., None]
    lower = jnp.where(strict, jnp.einsum('bhncd,bhnsd->bhncs', k_beta, k) * decay, 0.0)
    rhs = jnp.concatenate([v * beta[..., None], k_beta * jnp.exp(g)[..., None]], axis=-1)
    sol = lax.linalg.triangular_solve(lower, rhs, left_side=True, lower=True, unit_diagonal=True)
    u, w = sol[..., :dv], sol[..., dv:]
    intra = jnp.where(causal, jnp.einsum('bhncd,bhnsd->bhncs', q, k) * decay, 0.0)
    q_dec = q * jnp.exp(g)[..., None]
    k_dec = k * jnp.exp(g[..., -1:] - g)[..., None]
    chunk_decay = jnp.exp(g[..., -1])

    def step(state, inp):
        u_i, w_i, q_i, k_i, a_i, cd_i = inp
        v_new = u_i - jnp.einsum('bhcd,bhde->bhce', w_i, state)
        o_i = jnp.einsum('bhcd,bhde->bhce', q_i, state) + jnp.einsum('bhcs,bhse->bhce', a_i, v_new)
        state = state * cd_i[..., None, None] + jnp.einsum('bhcd,bhce->bhde', k_i, v_new)
        return state, o_i

    xs = tuple(jnp.moveaxis(t, 2, 0) for t in (u, w, q_dec, k_dec, intra, chunk_decay))
    state0 = jnp.zeros((b, h, dk, dv), jnp.float32)
    _, o = lax.scan(step, state0, xs)
    return jnp.transpose(o, (1, 0, 3, 2, 4)).reshape(b, l, h, dv)


def setup_inputs(seed: int = 0) -> dict:
    key = jax.random.key(seed)
    ks = jax.random.split(key, 20)
    f32 = jnp.float32

    def nrm(k, shape, s):
        return jax.random.normal(k, shape, f32) * s

    def gain(k, shape):
        return 1.0 + 0.02 * jax.random.normal(k, shape, f32)

    dt = jnp.exp(jax.random.uniform(ks[10], (DEPTH, DN_HEADS), f32, math.log(1e-3), math.log(1e-1)))
    return {
        'x': nrm(ks[0], (BATCH, SEQ, D_MODEL), 1.0),
        'c': nrm(ks[1], (BATCH, D_MODEL), 1.0),
        'ada_w': nrm(ks[2], (DEPTH, D_MODEL, N_MOD * D_MODEL), 0.5 * D_MODEL ** -0.5),
        'ada_b': nrm(ks[3], (DEPTH, N_MOD * D_MODEL), 0.02),
        'norm1_w': gain(ks[4], (DEPTH, D_MODEL)),
        'w_in': nrm(ks[5], (DEPTH, D_MODEL, P_IN), D_MODEL ** -0.5),
        'conv_a_w': nrm(ks[6], (DEPTH, A_CONV, A_WIDTH), A_CONV ** -0.5),
        'norm_a_w': gain(ks[7], (DEPTH, A_WIDTH)),
        'conv_qkv_w': nrm(ks[8], (DEPTH, DN_CONV, 3 * DN_WIDTH), DN_CONV ** -0.5),
        'a_log': jnp.log(jax.random.uniform(ks[9], (DEPTH, DN_HEADS), f32, 1.0, 16.0)),
        'dt_bias': dt + jnp.log(-jnp.expm1(-dt)),
        'norm_dn_w': gain(ks[11], (DEPTH, DN_HEAD_DIM)),
        'w_out': nrm(ks[12], (DEPTH, D_MIX, D_MODEL), D_MIX ** -0.5),
        'norm2_w': gain(ks[13], (DEPTH, D_MODEL)),
        'w_up': nrm(ks[14], (DEPTH, D_MODEL, 2 * D_FF), D_MODEL ** -0.5),
        'conv_ff_w': nrm(ks[15], (DEPTH, FF_CONV, 2 * D_FF), FF_CONV ** -0.5),
        'w_down': nrm(ks[16], (DEPTH, D_FF, D_MODEL), D_FF ** -0.5),
        'norm_f_w': gain(ks[17], (D_MODEL,)),
    }


def reference(x, c, ada_w, ada_b, norm1_w, w_in, conv_a_w, norm_a_w, conv_qkv_w, a_log,
              dt_bias, norm_dn_w, w_out, norm2_w, w_up, conv_ff_w, w_down, norm_f_w):
    b, l, _ = x.shape
    cut = np.cumsum([A_WIDTH, A_WIDTH, A_WIDTH, 3 * DN_WIDTH, DN_WIDTH, DN_HEADS]).tolist()
    c_act = jax.nn.silu(c)
    for i in range(DEPTH):
        mod = (c_act @ ada_w[i] + ada_b[i])[:, None, :]
        sh1, sc1, gt1, sh2, sc2, gt2 = jnp.split(mod, N_MOD, axis=-1)

        h = rmsnorm(x, norm1_w[i]) * (1.0 + sc1) + sh1
        p = h @ w_in[i]
        a_b, a_c, a_x, qkv, z, beta_logit, alpha = jnp.split(p, cut, axis=-1)

        ya = a_b * causal_dwconv(a_c * a_x, conv_a_w[i])
        ya = rmsnorm(ya.reshape(b, l, A_GROUPS, A_GROUP_DIM),
                     norm_a_w[i].reshape(A_GROUPS, A_GROUP_DIM)).reshape(b, l, A_WIDTH)

        qkv = jax.nn.silu(causal_dwconv(qkv, conv_qkv_w[i]))
        q, k, v = jnp.split(qkv.reshape(b, l, 3 * DN_HEADS, DN_HEAD_DIM), 3, axis=2)
        beta = jax.nn.sigmoid(beta_logit.astype(jnp.float32))
        g = -jnp.exp(a_log[i].astype(jnp.float32)) * jax.nn.softplus(
            alpha.astype(jnp.float32) + dt_bias[i].astype(jnp.float32))
        o = gated_delta_rule(l2norm(q), l2norm(k), v.astype(jnp.float32), g, beta).astype(x.dtype)
        o = rmsnorm(o, norm_dn_w[i]) * jax.nn.silu(z.reshape(b, l, DN_HEADS, DN_HEAD_DIM))
        yb = o.reshape(b, l, DN_WIDTH)

        y = jnp.concatenate([ya, yb], axis=-1) @ w_out[i]
        x = x + gt1 * y

        h = rmsnorm(x, norm2_w[i]) * (1.0 + sc2) + sh2
        gate, up = jnp.split(causal_dwconv(h @ w_up[i], conv_ff_w[i]), 2, axis=-1)
        x = x + gt2 * ((jax.nn.silu(gate) * up) @ w_down[i])
    return rmsnorm(x, norm_f_w)
```

```python
import functools

import jax
import jax.numpy as jnp
from jax import lax
from jax.experimental import pallas as pl
from jax.experimental.pallas import tpu as pltpu

F32 = jnp.float32
BF16 = jnp.bfloat16
EPS = 1e-6

A_GROUP_DIM = 64
DN_HEADS = 4
DN_HEAD_DIM = 128
CHUNK = 64
N_MOD = 6
HALO = 8
LANES = 128
FF_BLK = 256
VMEM_CAP = 60 * 1024 * 1024


def _vmem_limit(nbytes):
    return int(min(VMEM_CAP, nbytes))


def _parts(a, n):
    out = []
    r = a
    for i in range(n):
        p = r.astype(BF16)
        out.append(p)
        if i + 1 < n:
            r = r - p.astype(F32)
    return out


def _mm(a, b, pa=1, pb=1, dims=(((1,), (0,)), ((), ()))):
    ap = _parts(a, pa) if a.dtype != BF16 else [a]
    bp = _parts(b, pb) if b.dtype != BF16 else [b]
    order = max(len(ap), len(bp))
    acc = None
    for i in range(len(ap)):
        for j in range(len(bp)):
            if i + j < order:
                t = lax.dot_general(ap[i], bp[j], dims, preferred_element_type=F32)
                acc = t if acc is None else acc + t
    return acc


_NT = (((1,), (1,)), ((), ()))
_TN = (((0,), (0,)), ((), ()))


def _sigmoid(x):
    return 1.0 / (1.0 + jnp.exp(-x))


def _silu(x):
    return x * _sigmoid(x)


def _softplus(x):
    return jnp.maximum(x, 0.0) + jnp.log1p(jnp.exp(-jnp.abs(x)))


def _norm_mod(x, nw, sc, sh):
    ms = jnp.mean(x * x, axis=-1, keepdims=True)
    return x * lax.rsqrt(ms + EPS) * (nw * (1.0 + sc)) + sh


def _shift_rows(cur, prev, s):
    rolled = pltpu.roll(cur, s, axis=0)
    prolled = pltpu.roll(prev, s, axis=0)
    rid = lax.broadcasted_iota(jnp.int32, prev.shape, 0)
    top = jnp.where(rid < s, prolled, rolled[0:HALO])
    return jnp.concatenate([top, rolled[HALO:]], axis=0)


def _causal_conv(cur, halo_ref, w, width):
    prev = halo_ref[...]
    halo_ref[...] = cur[cur.shape[0] - HALO:, :]
    out = cur * w[width - 1:width, :]
    for s in range(1, width):
        out = out + _shift_rows(cur, prev, s) * w[width - 1 - s:width - s, :]
    return out


def _mod_kernel(c_ref, w_ref, b_ref, o_ref):
    c = c_ref[...]
    o_ref[0] = _mm(_silu(c), w_ref[0], 2, 2) + b_ref[0]


def _modulation(c, ada_w, ada_b):
    depth, d, nd = ada_w.shape
    b = c.shape[0]
    nj = nd // d
    out = pl.pallas_call(
        _mod_kernel,
        grid=(depth, nj),
        in_specs=[
            pl.BlockSpec((b, d), lambda i, j: (0, 0)),
            pl.BlockSpec((1, d, d), lambda i, j: (i, 0, j)),
            pl.BlockSpec((1, 1, d), lambda i, j: (i, 0, j)),
        ],
        out_specs=pl.BlockSpec((1, b, d), lambda i, j: (i, 0, j)),
        out_shape=jax.ShapeDtypeStruct((depth, b, nd), F32),
        name="adaln_mod",
    )(c, ada_w, ada_b.reshape(depth, 1, nd))
    return out.reshape(depth, b, nj, d)


def _inproj_kernel(x_ref, mod_ref, nw_ref, win_ref, wg_ref, caw_ref, naw_ref, gmat_ref, cqw_ref,
                   alog_ref, dtb_ref,
                   ya_ref, q_ref, k_ref, v_ref, zs_ref, gb_ref,
                   h_scr, halo_a, halo_q, *, aw, dnw):
    t = pl.program_id(1)

    @pl.when(t == 0)
    def _():
        halo_a[...] = jnp.zeros_like(halo_a)
        halo_q[...] = jnp.zeros_like(halo_q)

    m = mod_ref[0]
    h = _norm_mod(x_ref[0], nw_ref[...], m[1:2], m[0:1])
    h_scr[...] = h.astype(BF16)
    hb = h_scr[...]

    a_b = _mm(hb, win_ref[:, 0:aw])
    a_c = _mm(hb, win_ref[:, aw:2 * aw])
    a_x = _mm(hb, win_ref[:, 2 * aw:3 * aw])
    ya = a_b * _causal_conv(a_c * a_x, halo_a, caw_ref[...], 3)
    ss = _mm(ya * ya, gmat_ref[...], 2, 1)
    ya = ya * lax.rsqrt(ss * (1.0 / A_GROUP_DIM) + EPS) * naw_ref[...]
    ya_ref[0] = ya.astype(ya_ref.dtype)

    base = 3 * aw
    for part, oref in enumerate((q_ref, k_ref, v_ref)):
        raw = _mm(hb, win_ref[:, base + part * dnw:base + (part + 1) * dnw])
        cv = _causal_conv(raw, halo_q.at[part], cqw_ref[:, part * dnw:(part + 1) * dnw], 4)
        s = _silu(cv)
        if part < 2:
            for hh in range(DN_HEADS):
                seg = s[:, hh * DN_HEAD_DIM:(hh + 1) * DN_HEAD_DIM]
                n2 = jnp.sum(seg * seg, axis=-1, keepdims=True)
                seg = seg * lax.rsqrt(n2 + EPS)
                if part == 0:
                    seg = seg * (DN_HEAD_DIM ** -0.5)
                oref[0, :, hh * DN_HEAD_DIM:(hh + 1) * DN_HEAD_DIM] = seg
        else:
            oref[0] = s

    z = _mm(hb, win_ref[:, base + 3 * dnw:base + 4 * dnw])
    zs_ref[0] = _silu(z).astype(zs_ref.dtype)

    gates = _mm(hb, wg_ref[...])
    lane = lax.broadcasted_iota(jnp.int32, gates.shape, 1)
    g = -jnp.exp(alog_ref[...]) * _softplus(gates + dtb_ref[...])
    gb_ref[0] = jnp.where(lane < DN_HEADS, _sigmoid(gates), g)


def _inproj(x, mod_i, norm1_w, win_main, w_gates, conv_a_w, norm_a_w, gmat, conv_qkv_w, alog_pad, dtb_pad, *, tm):
    b, l, d = x.shape
    aw = conv_a_w.shape[-1]
    dnw = conv_qkv_w.shape[-1] // 3
    pm = win_main.shape[-1]
    nt = l // tm
    row = lambda bi, ti: (bi, ti, 0)
    fix2 = lambda bi, ti: (0, 0)
    est = (2 * tm * d * 4 + pm * d * 2 * 2 + tm * d * 2 + 2 * (2 * tm * aw * 2 + 3 * tm * dnw * 4 + tm * LANES * 4)
           + 10 * tm * dnw * 4 + 8 * 1024 * 1024)
    kern = functools.partial(_inproj_kernel, aw=aw, dnw=dnw)
    return pl.pallas_call(
        kern,
        grid=(b, nt),
        in_specs=[
            pl.BlockSpec((1, tm, d), row),
            pl.BlockSpec((1, N_MOD, d), lambda bi, ti: (bi, 0, 0)),
            pl.BlockSpec((1, d), fix2),
            pl.BlockSpec((d, pm), fix2, pipeline_mode=pl.Buffered(1)),
            pl.BlockSpec((d, LANES), fix2),
            pl.BlockSpec(conv_a_w.shape, fix2),
            pl.BlockSpec((1, aw), fix2),
            pl.BlockSpec((aw, aw), fix2),
            pl.BlockSpec(conv_qkv_w.shape, fix2),
            pl.BlockSpec((1, LANES), fix2),
            pl.BlockSpec((1, LANES), fix2),
        ],
        out_specs=[
            pl.BlockSpec((1, tm, aw), row),
            pl.BlockSpec((1, tm, dnw), row),
            pl.BlockSpec((1, tm, dnw), row),
            pl.BlockSpec((1, tm, dnw), row),
            pl.BlockSpec((1, tm, dnw), row),
            pl.BlockSpec((1, tm, LANES), row),
        ],
        out_shape=[
            jax.ShapeDtypeStruct((b, l, aw), BF16),
            jax.ShapeDtypeStruct((b, l, dnw), F32),
            jax.ShapeDtypeStruct((b, l, dnw), F32),
            jax.ShapeDtypeStruct((b, l, dnw), F32),
            jax.ShapeDtypeStruct((b, l, dnw), BF16),
            jax.ShapeDtypeStruct((b, l, LANES), F32),
        ],
        scratch_shapes=[
            pltpu.VMEM((tm, d), BF16),
            pltpu.VMEM((HALO, aw), F32),
            pltpu.VMEM((3, HALO, dnw), F32),
        ],
        compiler_params=pltpu.CompilerParams(
            dimension_semantics=("arbitrary", "arbitrary"), vmem_limit_bytes=_vmem_limit(est)),
        name="inproj",
    )(x, mod_i, norm1_w, win_main, w_gates, conv_a_w, norm_a_w, gmat, conv_qkv_w, alog_pad, dtb_pad)


_BNN = (((2,), (1,)), ((0,), (0,)))
_BNT = (((2,), (2,)), ((0,), (0,)))
_BTN = (((1,), (1,)), ((0,), (0,)))


def _inv_unit_lower(lmat, ri, ci, eye, p):
    def blk(v, s):
        return lax.shift_right_logical(v, jnp.int32(s.bit_length() - 1))
    t = eye - jnp.where(blk(ri, 2) == blk(ci, 2), lmat, 0.0)
    s = 4
    while s <= CHUNK:
        half = s // 2
        e = jnp.where((blk(ri, s) == blk(ci, s)) & (blk(ri, half) != blk(ci, half)), lmat, 0.0)
        t = t - _mm(t, _mm(e, t, p, p, _BNN), p, p, _BNN)
        s *= 2
    return t


def _delta_kernel(q_ref, k_ref, v_ref, gb_ref, o_ref, s_scr, *, nchunk, p):
    t = pl.program_id(1)

    @pl.when(t == 0)
    def _():
        s_scr[...] = jnp.zeros_like(s_scr)

    c_len, dk, nh = CHUNK, DN_HEAD_DIM, DN_HEADS
    n = nchunk * nh
    ri = lax.broadcasted_iota(jnp.int32, (n, c_len, c_len), 1)
    ci = lax.broadcasted_iota(jnp.int32, (n, c_len, c_len), 2)
    causal = ri >= ci
    strict = ri > ci
    eye = (ri == ci).astype(F32)
    tri = causal.astype(BF16)
    slow = strict.astype(F32)

    def gather(ref):
        return jnp.stack([ref[0, c * c_len:(c + 1) * c_len, hh * dk:(hh + 1) * dk]
                          for c in range(nchunk) for hh in range(nh)])

    q, k, v = gather(q_ref), gather(k_ref), gather(v_ref)
    gbt = jnp.stack([gb_ref[0, c * c_len:(c + 1) * c_len, :] for c in range(nchunk)])
    gcum = _mm(tri[:nchunk], gbt, 1, 3, _BNN)
    col = lambda a, c, j: a[c, :, j:j + 1]
    beta = jnp.stack([col(gbt, c, hh) for c in range(nchunk) for hh in range(nh)])
    g = jnp.stack([col(gbt, c, nh + hh) for c in range(nchunk) for hh in range(nh)])
    gc = jnp.stack([col(gcum, c, nh + hh) for c in range(nchunk) for hh in range(nh)])
    glast = gc[:, c_len - 1:c_len, :]

    gdiff = _mm(tri, g * slow, 1, 3, _BNN)
    decay = jnp.where(causal, jnp.exp(gdiff), 0.0)
    eg = jnp.exp(gc)
    kb = k * beta
    lmat = jnp.where(strict, _mm(kb, k, p, p, _BNT) * decay, 0.0)
    amat = _mm(q, k, p, p, _BNT) * decay
    tinv = _inv_unit_lower(lmat, ri, ci, eye, p)
    sol = _mm(tinv, jnp.concatenate([v * beta, kb * eg], axis=2), p, p, _BNN)
    u = sol[:, :, :dk]
    wq = jnp.concatenate([sol[:, :, dk:], q * eg], axis=1)
    kd = k * jnp.exp(glast - gc)
    cd = jnp.exp(glast)

    s = s_scr[...]
    for c in range(nchunk):
        sl = slice(c * nh, (c + 1) * nh)
        r = _mm(wq[sl], s, p, p, _BNN)
        v_new = u[sl] - r[:, :c_len]
        o = r[:, c_len:] + _mm(amat[sl], v_new, p, p, _BNN)
        s = s * cd[sl] + _mm(kd[sl], v_new, p, p, _BTN)
        for hh in range(nh):
            o_ref[0, c * c_len:(c + 1) * c_len, hh * dk:(hh + 1) * dk] = o[hh]
    s_scr[...] = s


def _delta(q, k, v, gb, *, tm, passes):
    b, l, dnw = q.shape
    nt = l // tm
    row = lambda bi, ti: (bi, ti, 0)
    est = 2 * (4 * tm * dnw * 4 + tm * LANES * 4) + DN_HEADS * DN_HEAD_DIM * DN_HEAD_DIM * 4 + 24 * 1024 * 1024
    kern = functools.partial(_delta_kernel, nchunk=tm // CHUNK, p=passes)
    return pl.pallas_call(
        kern,
        grid=(b, nt),
        in_specs=[pl.BlockSpec((1, tm, dnw), row)] * 3 + [pl.BlockSpec((1, tm, LANES), row)],
        out_specs=pl.BlockSpec((1, tm, dnw), row),
        out_shape=jax.ShapeDtypeStruct((b, l, dnw), F32),
        scratch_shapes=[pltpu.VMEM((DN_HEADS, DN_HEAD_DIM, DN_HEAD_DIM), F32)],
        compiler_params=pltpu.CompilerParams(
            dimension_semantics=("arbitrary", "arbitrary"), vmem_limit_bytes=_vmem_limit(est)),
        name="delta_rule",
    )(q, k, v, gb)


def _outproj_kernel(x_ref, mod_ref, ya_ref, o_ref, zs_ref, ndw_ref, wout_ref, y_ref, *, aw):
    m = mod_ref[0]
    acc = _mm(ya_ref[0], wout_ref[0:aw, :])
    o = o_ref[0]
    ndw = ndw_ref[...]
    for hh in range(DN_HEADS):
        cs = slice(hh * DN_HEAD_DIM, (hh + 1) * DN_HEAD_DIM)
        seg = o[:, cs]
        ms = jnp.mean(seg * seg, axis=-1, keepdims=True)
        yb = seg * lax.rsqrt(ms + EPS) * ndw * zs_ref[0, :, cs].astype(F32)
        acc = acc + _mm(yb, wout_ref[aw + hh * DN_HEAD_DIM:aw + (hh + 1) * DN_HEAD_DIM, :])
    y_ref[0] = x_ref[0] + m[2:3] * acc


def _outproj(x, mod_i, ya, o, zs, norm_dn_w, wout, *, tm):
    b, l, d = x.shape
    aw = ya.shape[-1]
    dnw = o.shape[-1]
    nt = l // tm
    row = lambda bi, ti: (bi, ti, 0)
    fix2 = lambda bi, ti: (0, 0)
    est = 2 * (2 * tm * d * 4 + tm * aw * 2 + tm * dnw * 4 + tm * dnw * 2) + 2 * d * d * 2 + 6 * tm * d * 4
    kern = functools.partial(_outproj_kernel, aw=aw)
    return pl.pallas_call(
        kern,
        grid=(b, nt),
        in_specs=[
            pl.BlockSpec((1, tm, d), row),
            pl.BlockSpec((1, N_MOD, d), lambda bi, ti: (bi, 0, 0)),
            pl.BlockSpec((1, tm, aw), row),
            pl.BlockSpec((1, tm, dnw), row),
            pl.BlockSpec((1, tm, dnw), row),
            pl.BlockSpec((1, DN_HEAD_DIM), fix2),
            pl.BlockSpec(wout.shape, fix2),
        ],
        out_specs=pl.BlockSpec((1, tm, d), row),
        out_shape=jax.ShapeDtypeStruct((b, l, d), F32),
        compiler_params=pltpu.CompilerParams(
            dimension_semantics=("arbitrary", "arbitrary"), vmem_limit_bytes=_vmem_limit(est)),
        name="outproj",
    )(x, mod_i, ya, o, zs, norm_dn_w, wout)


def _ffn_kernel(x_ref, mod_ref, nw_ref, wup_ref, cw_ref, wdn_ref, nf_ref, y_ref,
                h_scr, acc_scr, halo_scr, *, nblk, final):
    t = pl.program_id(1)

    @pl.when(t == 0)
    def _():
        halo_scr[...] = jnp.zeros_like(halo_scr)

    m = mod_ref[0]
    x = x_ref[0]
    h = _norm_mod(x, nw_ref[...], m[4:5], m[3:4])
    h_scr[...] = h.astype(BF16)
    acc_scr[...] = jnp.zeros_like(acc_scr)

    def body(j, carry):
        u = _mm(h_scr[...], wup_ref[j])
        cv = _causal_conv(u, halo_scr.at[j], cw_ref[j], 3)
        act = _silu(cv[:, :FF_BLK]) * cv[:, FF_BLK:]
        acc_scr[...] += _mm(act.astype(BF16), wdn_ref[j])
        return carry

    lax.fori_loop(0, nblk, body, 0)
    y = x + m[5:6] * acc_scr[...]
    if final:
        ms = jnp.mean(y * y, axis=-1, keepdims=True)
        y = y * lax.rsqrt(ms + EPS) * nf_ref[...]
    y_ref[0] = y


def _ffn(x, mod_i, norm2_w, wup_blk, cw_blk, wdn_blk, norm_f_w, *, tm, final):
    b, l, d = x.shape
    nblk = wup_blk.shape[0]
    nt = l // tm
    row = lambda bi, ti: (bi, ti, 0)
    fix2 = lambda bi, ti: (0, 0)
    fix3 = lambda bi, ti: (0, 0, 0)
    est = (4 * tm * d * 4 + wup_blk.size * 2 + wdn_blk.size * 2 + tm * d * 2 + tm * d * 4
           + nblk * HALO * 2 * FF_BLK * 4 + 10 * tm * 2 * FF_BLK * 4 + 4 * 1024 * 1024)
    kern = functools.partial(_ffn_kernel, nblk=nblk, final=final)
    return pl.pallas_call(
        kern,
        grid=(b, nt),
        in_specs=[
            pl.BlockSpec((1, tm, d), row),
            pl.BlockSpec((1, N_MOD, d), lambda bi, ti: (bi, 0, 0)),
            pl.BlockSpec((1, d), fix2),
            pl.BlockSpec(wup_blk.shape, fix3, pipeline_mode=pl.Buffered(1)),
            pl.BlockSpec(cw_blk.shape, fix3),
            pl.BlockSpec(wdn_blk.shape, fix3, pipeline_mode=pl.Buffered(1)),
            pl.BlockSpec((1, d), fix2),
        ],
        out_specs=pl.BlockSpec((1, tm, d), row),
        out_shape=jax.ShapeDtypeStruct((b, l, d), F32),
        scratch_shapes=[
            pltpu.VMEM((tm, d), BF16),
            pltpu.VMEM((tm, d), F32),
            pltpu.VMEM((nblk, HALO, 2 * FF_BLK), F32),
        ],
        compiler_params=pltpu.CompilerParams(
            dimension_semantics=("arbitrary", "arbitrary"), vmem_limit_bytes=_vmem_limit(est)),
        name="ffn",
    )(x, mod_i, norm2_w, wup_blk, cw_blk, wdn_blk, norm_f_w)


def _pad_rows(w, rows):
    return jnp.pad(w, ((0, rows - w.shape[0]), (0, 0)))


def kernel(x, c, ada_w, ada_b, norm1_w, w_in, conv_a_w, norm_a_w, conv_qkv_w, a_log, dt_bias, norm_dn_w,
           w_out, norm2_w, w_up, conv_ff_w, w_down, norm_f_w):
    b, l, d = x.shape
    depth = ada_w.shape[0]
    aw = conv_a_w.shape[-1]
    dnw = conv_qkv_w.shape[-1] // 3
    d_ff = w_down.shape[1]
    pm = 3 * aw + 4 * dnw
    nblk = d_ff // FF_BLK
    assert d_ff % FF_BLK == 0 and dnw == DN_HEADS * DN_HEAD_DIM and w_in.shape[-1] == pm + 2 * DN_HEADS
    tm = min(512, l)
    tm_delta = min(256, l)
    assert l % tm == 0 and l % tm_delta == 0 and tm_delta % CHUNK == 0

    mod = _modulation(c, ada_w, ada_b)

    gid = jnp.arange(aw) // A_GROUP_DIM
    gmat = (gid[:, None] == gid[None, :]).astype(BF16)

    for i in range(depth):
        win_main = w_in[i, :, :pm].astype(BF16)
        w_gates = jnp.pad(w_in[i, :, pm:], ((0, 0), (0, LANES - 2 * DN_HEADS))).astype(BF16)
        alog_pad = jnp.pad(a_log[i][None, :], ((0, 0), (DN_HEADS, LANES - 2 * DN_HEADS)))
        dtb_pad = jnp.pad(dt_bias[i][None, :], ((0, 0), (DN_HEADS, LANES - 2 * DN_HEADS)))
        ya, q, k, v, zs, gb = _inproj(
            x, mod[i], norm1_w[i][None, :], win_main, w_gates, _pad_rows(conv_a_w[i], HALO),
            norm_a_w[i][None, :], gmat, _pad_rows(conv_qkv_w[i], HALO), alog_pad, dtb_pad, tm=tm)
        o = _delta(q, k, v, gb, tm=tm_delta, passes=1)
        x = _outproj(x, mod[i], ya, o, zs, norm_dn_w[i][None, :], w_out[i].astype(BF16), tm=tm)

        wu = w_up[i]
        wup_blk = jnp.concatenate(
            [wu[:, :d_ff].reshape(d, nblk, FF_BLK), wu[:, d_ff:].reshape(d, nblk, FF_BLK)], axis=-1)
        wup_blk = jnp.transpose(wup_blk, (1, 0, 2)).astype(BF16)
        cw = conv_ff_w[i]
        cw_blk = jnp.concatenate(
            [cw[:, :d_ff].reshape(-1, nblk, FF_BLK), cw[:, d_ff:].reshape(-1, nblk, FF_BLK)], axis=-1)
        cw_blk = jnp.pad(jnp.transpose(cw_blk, (1, 0, 2)), ((0, 0), (0, HALO - cw.shape[0]), (0, 0)))
        wdn_blk = w_down[i].reshape(nblk, FF_BLK, d).astype(BF16)
        x = _ffn(x, mod[i], norm2_w[i][None, :], wup_blk, cw_blk, wdn_blk, norm_f_w[None, :],
                 tm=tm, final=(i == depth - 1))
    return x
```

```python
import functools

import jax
import jax.numpy as jnp
from jax import lax
from jax.experimental import pallas as pl
from jax.experimental.pallas import tpu as pltpu

F32 = jnp.float32
BF16 = jnp.bfloat16
EPS = 1e-6

A_GROUP_DIM = 64
DN_HEADS = 4
DN_HEAD_DIM = 128
CHUNK = 64
N_MOD = 6
HALO = 8
LANES = 128
FF_BLK = 256
VMEM_CAP = 60 * 1024 * 1024


def _vmem_limit(nbytes):
    return int(min(VMEM_CAP, nbytes))


def _parts(a, n):
    out = []
    r = a
    for i in range(n):
        p = r.astype(BF16)
        out.append(p)
        if i + 1 < n:
            r = r - p.astype(F32)
    return out


def _mm(a, b, pa=1, pb=1, dims=(((1,), (0,)), ((), ()))):
    ap = _parts(a, pa) if a.dtype != BF16 else [a]
    bp = _parts(b, pb) if b.dtype != BF16 else [b]
    order = max(len(ap), len(bp))
    acc = None
    for i in range(len(ap)):
        for j in range(len(bp)):
            if i + j < order:
                t = lax.dot_general(ap[i], bp[j], dims, preferred_element_type=F32)
                acc = t if acc is None else acc + t
    return acc


_NT = (((1,), (1,)), ((), ()))
_TN = (((0,), (0,)), ((), ()))


def _sigmoid(x):
    return 0.5 * jnp.tanh(0.5 * x) + 0.5


def _silu(x):
    h = 0.5 * x
    return h * jnp.tanh(h) + h


def _softplus(x):
    return jnp.maximum(x, 0.0) + jnp.log1p(jnp.exp(-jnp.abs(x)))


def _norm_mod(x, nw, sc, sh):
    ms = jnp.mean(x * x, axis=-1, keepdims=True)
    return x * lax.rsqrt(ms + EPS) * (nw * (1.0 + sc)) + sh


def _shift_rows(cur, prev, s):
    rolled = pltpu.roll(cur, s, axis=0)
    prolled = pltpu.roll(prev, s, axis=0)
    rid = lax.broadcasted_iota(jnp.int32, prev.shape, 0)
    top = jnp.where(rid < s, prolled, rolled[0:HALO])
    return jnp.concatenate([top, rolled[HALO:]], axis=0)


def _causal_conv(cur, halo_ref, w, width):
    prev = halo_ref[...]
    halo_ref[...] = cur[cur.shape[0] - HALO:, :]
    out = cur * w[width - 1:width, :]
    for s in range(1, width):
        out = out + _shift_rows(cur, prev, s) * w[width - 1 - s:width - s, :]
    return out


def _mod_kernel(c_ref, w_ref, b_ref, o_ref):
    c = c_ref[...]
    o_ref[0] = _mm(_silu(c), w_ref[0], 2, 2) + b_ref[0]


def _modulation(c, ada_w, ada_b):
    depth, d, nd = ada_w.shape
    b = c.shape[0]
    nj = nd // d
    out = pl.pallas_call(
        _mod_kernel,
        grid=(depth, nj),
        in_specs=[
            pl.BlockSpec((b, d), lambda i, j: (0, 0)),
            pl.BlockSpec((1, d, d), lambda i, j: (i, 0, j)),
            pl.BlockSpec((1, 1, d), lambda i, j: (i, 0, j)),
        ],
        out_specs=pl.BlockSpec((1, b, d), lambda i, j: (i, 0, j)),
        out_shape=jax.ShapeDtypeStruct((depth, b, nd), F32),
        name="adaln_mod",
    )(c, ada_w, ada_b.reshape(depth, 1, nd))
    return out.reshape(depth, b, nj, d)


def _inproj_kernel(x_ref, mod_ref, nw_ref, win_ref, wg_ref, caw_ref, naw_ref, gmat_ref, cqw_ref,
                   alog_ref, dtb_ref,
                   ya_ref, q_ref, k_ref, v_ref, zs_ref, gb_ref,
                   h_scr, halo_a, halo_q, *, aw, dnw):
    t = pl.program_id(1)

    @pl.when(t == 0)
    def _():
        halo_a[...] = jnp.zeros_like(halo_a)
        halo_q[...] = jnp.zeros_like(halo_q)

    m = mod_ref[0]
    h = _norm_mod(x_ref[0], nw_ref[...], m[1:2], m[0:1])
    h_scr[...] = h.astype(BF16)
    hb = h_scr[...]

    base = 3 * aw

    def proj(lo, width):
        return _mm(hb, win_ref[:, lo:lo + width])

    def mixer_a(a_b, a_c, a_x):
        ya = a_b * _causal_conv(a_c * a_x, halo_a, caw_ref[...], 3)
        ss = _mm(ya * ya, gmat_ref[...], 2, 1)
        ya = ya * lax.rsqrt(ss * (1.0 / A_GROUP_DIM) + EPS) * naw_ref[...]
        ya_ref[0] = ya.astype(ya_ref.dtype)

    def qkv_part(part, raw, oref):
        cv = _causal_conv(raw, halo_q.at[part], cqw_ref[:, part * dnw:(part + 1) * dnw], 4)
        s = _silu(cv)
        for hh in range(DN_HEADS):
            seg = s[:, hh * DN_HEAD_DIM:(hh + 1) * DN_HEAD_DIM]
            if part < 2:
                n2 = jnp.sum(seg * seg, axis=-1, keepdims=True)
                seg = seg * lax.rsqrt(n2 + EPS)
            if part == 0:
                seg = seg * (DN_HEAD_DIM ** -0.5)
            oref[0, :, hh] = seg.reshape(-1, CHUNK, DN_HEAD_DIM)

    a_b, a_c, a_x = proj(0, aw), proj(aw, aw), proj(2 * aw, aw)
    raw_q = proj(base, dnw)
    mixer_a(a_b, a_c, a_x)
    raw_k = proj(base + dnw, dnw)
    qkv_part(0, raw_q, q_ref)
    raw_v = proj(base + 2 * dnw, dnw)
    qkv_part(1, raw_k, k_ref)
    z = proj(base + 3 * dnw, dnw)
    qkv_part(2, raw_v, v_ref)
    gates = _mm(hb, wg_ref[...])
    zs_ref[0] = _silu(z).astype(zs_ref.dtype)

    lane = lax.broadcasted_iota(jnp.int32, gates.shape, 1)
    g = -jnp.exp(alog_ref[...]) * _softplus(gates + dtb_ref[...])
    gb_ref[0] = jnp.where(lane < DN_HEADS, _sigmoid(gates), g)


def _inproj(x, mod_i, norm1_w, win_main, w_gates, conv_a_w, norm_a_w, gmat, conv_qkv_w, alog_pad, dtb_pad, *, tm):
    b, l, d = x.shape
    aw = conv_a_w.shape[-1]
    dnw = conv_qkv_w.shape[-1] // 3
    pm = win_main.shape[-1]
    nt = l // tm
    row = lambda bi, ti: (bi, ti, 0)
    fix2 = lambda bi, ti: (0, 0)
    est = (2 * tm * d * 4 + pm * d * 2 * 2 + tm * d * 2 + 2 * (2 * tm * aw * 2 + 3 * tm * dnw * 4 + tm * LANES * 4)
           + 10 * tm * dnw * 4 + 8 * 1024 * 1024)
    ch_shape = (b, l // CHUNK, DN_HEADS, CHUNK, DN_HEAD_DIM)
    ch_spec = pl.BlockSpec((1, tm // CHUNK, DN_HEADS, CHUNK, DN_HEAD_DIM), lambda bi, ti: (bi, ti, 0, 0, 0))
    kern = functools.partial(_inproj_kernel, aw=aw, dnw=dnw)
    return pl.pallas_call(
        kern,
        grid=(b, nt),
        in_specs=[
            pl.BlockSpec((1, tm, d), row),
            pl.BlockSpec((1, N_MOD, d), lambda bi, ti: (bi, 0, 0)),
            pl.BlockSpec((1, d), fix2),
            pl.BlockSpec((d, pm), fix2, pipeline_mode=pl.Buffered(1)),
            pl.BlockSpec((d, LANES), fix2),
            pl.BlockSpec(conv_a_w.shape, fix2),
            pl.BlockSpec((1, aw), fix2),
            pl.BlockSpec((aw, aw), fix2),
            pl.BlockSpec(conv_qkv_w.shape, fix2),
            pl.BlockSpec((1, LANES), fix2),
            pl.BlockSpec((1, LANES), fix2),
        ],
        out_specs=[
            pl.BlockSpec((1, tm, aw), row),
            ch_spec,
            ch_spec,
            ch_spec,
            pl.BlockSpec((1, tm, dnw), row),
            pl.BlockSpec((1, tm, LANES), row),
        ],
        out_shape=[
            jax.ShapeDtypeStruct((b, l, aw), BF16),
            jax.ShapeDtypeStruct(ch_shape, F32),
            jax.ShapeDtypeStruct(ch_shape, F32),
            jax.ShapeDtypeStruct(ch_shape, F32),
            jax.ShapeDtypeStruct((b, l, dnw), BF16),
            jax.ShapeDtypeStruct((b, l, LANES), F32),
        ],
        scratch_shapes=[
            pltpu.VMEM((tm, d), BF16),
            pltpu.VMEM((HALO, aw), F32),
            pltpu.VMEM((3, HALO, dnw), F32),
        ],
        compiler_params=pltpu.CompilerParams(
            dimension_semantics=("arbitrary", "arbitrary"), vmem_limit_bytes=_vmem_limit(est)),
        name="inproj",
    )(x, mod_i, norm1_w, win_main, w_gates, conv_a_w, norm_a_w, gmat, conv_qkv_w, alog_pad, dtb_pad)


_BNN = (((2,), (1,)), ((0,), (0,)))
_BNT = (((2,), (2,)), ((0,), (0,)))
_BTN = (((1,), (1,)), ((0,), (0,)))


def _delta_kernel(q_ref, k_ref, v_ref, gb_ref, o_ref, s_scr, u_scr, wq_scr, a_scr, kd_scr, cd_scr,
                  *, nchunk, nt, p):
    g_id = pl.program_id(0)
    wslot = lax.rem(g_id, 2)
    rslot = 1 - wslot

    @pl.when(g_id == 0)
    def _():
        for ref in (s_scr, u_scr, wq_scr, a_scr, kd_scr, cd_scr):
            ref[...] = jnp.zeros_like(ref)

    c_len, dk, nh = CHUNK, DN_HEAD_DIM, DN_HEADS
    n = nchunk * nh
    ri = lax.broadcasted_iota(jnp.int32, (n, c_len, c_len), 1)
    ci = lax.broadcasted_iota(jnp.int32, (n, c_len, c_len), 2)
    causal = ri >= ci
    strict = ri > ci

    def blk(v, s):
        return lax.shift_right_logical(v, jnp.int32(s.bit_length() - 1))

    def prepare():
        q, k, v = (ref[0].reshape(n, c_len, dk) for ref in (q_ref, k_ref, v_ref))
        gbt = jnp.stack([gb_ref[0, c * c_len:(c + 1) * c_len, :] for c in range(nchunk)])
        gcum = _mm(causal[:nchunk].astype(BF16), gbt, 1, 3, _BNN)
        gcum_t = _mm(gbt, (ri <= ci)[:nchunk].astype(BF16), 3, 1, _BTN)
        yield
        pick = [(c, hh) for c in range(nchunk) for hh in range(nh)]
        beta = jnp.stack([gbt[c, :, hh:hh + 1] for c, hh in pick])
        gc = jnp.stack([gcum[c, :, nh + hh:nh + hh + 1] for c, hh in pick])
        gcr = jnp.stack([gcum_t[c, nh + hh:nh + hh + 1, :] for c, hh in pick])
        glast = gc[:, c_len - 1:c_len, :]
        decay = jnp.where(causal, jnp.exp(gc - gcr), 0.0)
        eg = jnp.exp(gc)
        kb = k * beta
        qkk = _mm(jnp.concatenate([q, kb], axis=1), k, p, p, _BNT)
        yield
        amat = qkk[:, :c_len] * decay
        lmat = jnp.where(strict, qkk[:, c_len:] * decay, 0.0)
        tinv = (ri == ci).astype(F32) - jnp.where(blk(ri, 2) == blk(ci, 2), lmat, 0.0)
        s = 4
        while s <= c_len:
            e = jnp.where((blk(ri, s) == blk(ci, s)) & (blk(ri, s // 2) != blk(ci, s // 2)), lmat, 0.0)
            et = _mm(e, tinv, p, p, _BNN)
            yield
            tinv = tinv - _mm(tinv, et, p, p, _BNN)
            yield
            s *= 2
        sol = _mm(tinv, jnp.concatenate([v * beta, kb * eg], axis=2), p, p, _BNN)
        yield
        u_scr[wslot] = sol[:, :, :dk]
        wq_scr[wslot] = jnp.concatenate([sol[:, :, dk:], q * eg], axis=1)
        a_scr[wslot] = amat
        kd_scr[wslot] = k * jnp.exp(glast - gc)
        cd_scr[wslot] = jnp.broadcast_to(jnp.exp(glast), cd_scr.shape[1:])

    def recur():
        first = lax.rem(g_id - 1, nt) == 0
        s = jnp.where(first, 0.0, s_scr[...])
        for c in range(nchunk):
            sl = pl.ds(c * nh, nh)
            r = _mm(wq_scr[rslot, sl], s, p, p, _BNN)
            yield
            v_new = u_scr[rslot, sl] - r[:, :c_len]
            o = r[:, c_len:] + _mm(a_scr[rslot, sl], v_new, p, p, _BNN)
            s = s * cd_scr[rslot, sl] + _mm(kd_scr[rslot, sl], v_new, p, p, _BTN)
            o_ref[0, c] = o
            yield
        s_scr[...] = s

    gens = [recur(), prepare()]
    while gens:
        for gen in list(gens):
            if next(gen, "done") == "done":
                gens.remove(gen)


def _delta(q, k, v, gb, *, tm, passes):
    b, l = gb.shape[:2]
    dnw = DN_HEADS * DN_HEAD_DIM
    nt = l // tm
    nsteps = b * nt
    nchunk = tm // CHUNK
    n = nchunk * DN_HEADS
    dk = DN_HEAD_DIM

    def tile_in(g):
        t = jnp.minimum(g, nsteps - 1)
        return t // nt, t % nt

    def tile_out(g):
        t = jnp.maximum(g - 1, 0)
        return t // nt, t % nt

    ch_block = (1, nchunk, DN_HEADS, CHUNK, dk)

    scratch = [
        pltpu.VMEM((DN_HEADS, dk, dk), F32),
        pltpu.VMEM((2, n, CHUNK, dk), F32),
        pltpu.VMEM((2, n, 2 * CHUNK, dk), F32),
        pltpu.VMEM((2, n, CHUNK, CHUNK), F32),
        pltpu.VMEM((2, n, CHUNK, dk), F32),
        pltpu.VMEM((2, n, 1, dk), F32),
    ]
    est = (2 * (4 * tm * dnw * 4 + tm * LANES * 4) + 2 * n * (4 * CHUNK + 2 * CHUNK + HALO) * dk * 4
           + 40 * n * CHUNK * LANES * 4)
    kern = functools.partial(_delta_kernel, nchunk=nchunk, nt=nt, p=passes)
    return pl.pallas_call(
        kern,
        grid=(nsteps + 1,),
        in_specs=[pl.BlockSpec(ch_block, lambda g: (*tile_in(g), 0, 0, 0))] * 3
        + [pl.BlockSpec((1, tm, LANES), lambda g: (*tile_in(g), 0))],
        out_specs=pl.BlockSpec(ch_block, lambda g: (*tile_out(g), 0, 0, 0)),
        out_shape=jax.ShapeDtypeStruct(q.shape, F32),
        scratch_shapes=scratch,
        compiler_params=pltpu.CompilerParams(
            dimension_semantics=("arbitrary",), vmem_limit_bytes=_vmem_limit(est)),
        name="delta_rule",
    )(q, k, v, gb)


def _outproj_kernel(x_ref, mod_ref, ya_ref, o_ref, zs_ref, ndw_ref, wout_ref, y_ref, *, aw):
    m = mod_ref[0]
    ndw = ndw_ref[...]
    parts = [ya_ref[0]]
    for hh in range(DN_HEADS):
        seg = o_ref[0, :, hh].reshape(-1, DN_HEAD_DIM)
        ms = jnp.mean(seg * seg, axis=-1, keepdims=True)
        zs = zs_ref[0, :, hh * DN_HEAD_DIM:(hh + 1) * DN_HEAD_DIM].astype(F32)
        parts.append((seg * lax.rsqrt(ms + EPS) * ndw * zs).astype(BF16))
    y = _mm(jnp.concatenate(parts, axis=1), wout_ref[...])
    y_ref[0] = x_ref[0] + m[2:3] * y


def _outproj(x, mod_i, ya, o, zs, norm_dn_w, wout, *, tm):
    b, l, d = x.shape
    aw = ya.shape[-1]
    dnw = zs.shape[-1]
    nt = l // tm
    row = lambda bi, ti: (bi, ti, 0)
    fix2 = lambda bi, ti: (0, 0)
    est = 2 * (2 * tm * d * 4 + tm * aw * 2 + tm * dnw * 4 + tm * dnw * 2) + 2 * d * d * 2 + 6 * tm * d * 4
    kern = functools.partial(_outproj_kernel, aw=aw)
    return pl.pallas_call(
        kern,
        grid=(b, nt),
        in_specs=[
            pl.BlockSpec((1, tm, d), row),
            pl.BlockSpec((1, N_MOD, d), lambda bi, ti: (bi, 0, 0)),
            pl.BlockSpec((1, tm, aw), row),
            pl.BlockSpec((1, tm // CHUNK, DN_HEADS, CHUNK, DN_HEAD_DIM), lambda bi, ti: (bi, ti, 0, 0, 0)),
            pl.BlockSpec((1, tm, dnw), row),
            pl.BlockSpec((1, DN_HEAD_DIM), fix2),
            pl.BlockSpec(wout.shape, fix2),
        ],
        out_specs=pl.BlockSpec((1, tm, d), row),
        out_shape=jax.ShapeDtypeStruct((b, l, d), F32),
        compiler_params=pltpu.CompilerParams(
            dimension_semantics=("arbitrary", "arbitrary"), vmem_limit_bytes=_vmem_limit(est)),
        name="outproj",
    )(x, mod_i, ya, o, zs, norm_dn_w, wout)


def _ffn_kernel(x_ref, mod_ref, nw_ref, wup_ref, cw_ref, wdn_ref, nf_ref, y_ref,
                h_scr, acc_scr, halo_scr, u_scr, *, nblk, final):
    t = pl.program_id(1)

    @pl.when(t == 0)
    def _():
        halo_scr[...] = jnp.zeros_like(halo_scr)

    m = mod_ref[0]
    x = x_ref[0]
    h = _norm_mod(x, nw_ref[...], m[4:5], m[3:4])
    h_scr[...] = h.astype(BF16)
    acc_scr[...] = jnp.zeros_like(acc_scr)

    def up(j, slot):
        u_scr[slot] = _mm(h_scr[...], wup_ref[j])

    def down(j, slot):
        cv = _causal_conv(u_scr[slot], halo_scr.at[j], cw_ref[j], 3)
        act = _silu(cv[:, :FF_BLK]) * cv[:, FF_BLK:]
        acc_scr[...] += _mm(act.astype(BF16), wdn_ref[j])

    up(0, 0)

    def pair(i, carry):
        j = 2 * i
        up(j + 1, 1)
        down(j, 0)
        up(j + 2, 0)
        down(j + 1, 1)
        return carry

    lax.fori_loop(0, (nblk - 1) // 2, pair, 0)
    if nblk % 2 == 0:
        up(nblk - 1, 1)
        down(nblk - 2, 0)
        down(nblk - 1, 1)
    else:
        down(nblk - 1, 0)
    y = x + m[5:6] * acc_scr[...]
    if final:
        ms = jnp.mean(y * y, axis=-1, keepdims=True)
        y = y * lax.rsqrt(ms + EPS) * nf_ref[...]
    y_ref[0] = y


def _ffn(x, mod_i, norm2_w, wup_blk, cw_blk, wdn_blk, norm_f_w, *, tm, final):
    b, l, d = x.shape
    nblk = wup_blk.shape[0]
    nt = l // tm
    row = lambda bi, ti: (bi, ti, 0)
    fix2 = lambda bi, ti: (0, 0)
    fix3 = lambda bi, ti: (0, 0, 0)
    est = (4 * tm * d * 4 + wup_blk.size * 2 + wdn_blk.size * 2 + tm * d * 2 + tm * d * 4
           + nblk * HALO * 2 * FF_BLK * 4 + 10 * tm * 2 * FF_BLK * 4 + 4 * 1024 * 1024)
    kern = functools.partial(_ffn_kernel, nblk=nblk, final=final)
    return pl.pallas_call(
        kern,
        grid=(b, nt),
        in_specs=[
            pl.BlockSpec((1, tm, d), row),
            pl.BlockSpec((1, N_MOD, d), lambda bi, ti: (bi, 0, 0)),
            pl.BlockSpec((1, d), fix2),
            pl.BlockSpec(wup_blk.shape, fix3, pipeline_mode=pl.Buffered(1)),
            pl.BlockSpec(cw_blk.shape, fix3),
            pl.BlockSpec(wdn_blk.shape, fix3, pipeline_mode=pl.Buffered(1)),
            pl.BlockSpec((1, d), fix2),
        ],
        out_specs=pl.BlockSpec((1, tm, d), row),
        out_shape=jax.ShapeDtypeStruct((b, l, d), F32),
        scratch_shapes=[
            pltpu.VMEM((tm, d), BF16),
            pltpu.VMEM((tm, d), F32),
            pltpu.VMEM((nblk, HALO, 2 * FF_BLK), F32),
            pltpu.VMEM((2, tm, 2 * FF_BLK), F32),
        ],
        compiler_params=pltpu.CompilerParams(
            dimension_semantics=("arbitrary", "arbitrary"), vmem_limit_bytes=_vmem_limit(est)),
        name="ffn",
    )(x, mod_i, norm2_w, wup_blk, cw_blk, wdn_blk, norm_f_w)


def _pad_rows(w, rows):
    return jnp.pad(w, ((0, rows - w.shape[0]), (0, 0)))


def kernel(x, c, ada_w, ada_b, norm1_w, w_in, conv_a_w, norm_a_w, conv_qkv_w, a_log, dt_bias, norm_dn_w,
           w_out, norm2_w, w_up, conv_ff_w, w_down, norm_f_w):
    b, l, d = x.shape
    depth = ada_w.shape[0]
    aw = conv_a_w.shape[-1]
    dnw = conv_qkv_w.shape[-1] // 3
    d_ff = w_down.shape[1]
    pm = 3 * aw + 4 * dnw
    nblk = d_ff // FF_BLK
    assert d_ff % FF_BLK == 0 and dnw == DN_HEADS * DN_HEAD_DIM and w_in.shape[-1] == pm + 2 * DN_HEADS
    tm = min(512, l)
    tm_delta = min(512, l)
    assert l % tm == 0 and l % tm_delta == 0 and tm_delta % CHUNK == 0

    mod = _modulation(c, ada_w, ada_b)

    gid = jnp.arange(aw) // A_GROUP_DIM
    gmat = (gid[:, None] == gid[None, :]).astype(BF16)

    for i in range(depth):
        win_main = w_in[i, :, :pm].astype(BF16)
        w_gates = jnp.pad(w_in[i, :, pm:], ((0, 0), (0, LANES - 2 * DN_HEADS))).astype(BF16)
        alog_pad = jnp.pad(a_log[i][None, :], ((0, 0), (DN_HEADS, LANES - 2 * DN_HEADS)))
        dtb_pad = jnp.pad(dt_bias[i][None, :], ((0, 0), (DN_HEADS, LANES - 2 * DN_HEADS)))
        ya, q, k, v, zs, gb = _inproj(
            x, mod[i], norm1_w[i][None, :], win_main, w_gates, _pad_rows(conv_a_w[i], HALO),
            norm_a_w[i][None, :], gmat, _pad_rows(conv_qkv_w[i], HALO), alog_pad, dtb_pad, tm=tm)
        o = _delta(q, k, v, gb, tm=tm_delta, passes=1)
        x = _outproj(x, mod[i], ya, o, zs, norm_dn_w[i][None, :], w_out[i].astype(BF16), tm=tm)

        wu = w_up[i]
        wup_blk = jnp.concatenate(
            [wu[:, :d_ff].reshape(d, nblk, FF_BLK), wu[:, d_ff:].reshape(d, nblk, FF_BLK)], axis=-1)
        wup_blk = jnp.transpose(wup_blk, (1, 0, 2)).astype(BF16)
        cw = conv_ff_w[i]
        cw_blk = jnp.concatenate(
            [cw[:, :d_ff].reshape(-1, nblk, FF_BLK), cw[:, d_ff:].reshape(-1, nblk, FF_BLK)], axis=-1)
        cw_blk = jnp.pad(jnp.transpose(cw_blk, (1, 0, 2)), ((0, 0), (0, HALO - cw.shape[0]), (0, 0)))
        wdn_blk = w_down[i].reshape(nblk, FF_BLK, d).astype(BF16)
        x = _ffn(x, mod[i], norm2_w[i][None, :], wup_blk, cw_blk, wdn_blk, norm_f_w[None, :],
                 tm=tm, final=(i == depth - 1))
    return x
```

```python
import functools

import jax
import jax.numpy as jnp
from jax import lax
from jax.experimental import pallas as pl
from jax.experimental.pallas import tpu as pltpu

F32 = jnp.float32
BF16 = jnp.bfloat16
EPS = 1e-6

A_GROUP_DIM = 64
DN_HEADS = 4
DN_HEAD_DIM = 128
CHUNK = 64
N_MOD = 6
HALO = 8
LANES = 128
FF_BLK = 256
STRIP = 32
VMEM_CAP = 60 * 1024 * 1024


def _vmem_limit(nbytes):
    return int(min(VMEM_CAP, nbytes))


def _parts(a, n):
    out = []
    r = a
    for i in range(n):
        p = r.astype(BF16)
        out.append(p)
        if i + 1 < n:
            r = r - p.astype(F32)
    return out


def _mm(a, b, pa=1, pb=1, dims=(((1,), (0,)), ((), ()))):
    ap = _parts(a, pa) if a.dtype != BF16 else [a]
    bp = _parts(b, pb) if b.dtype != BF16 else [b]
    order = max(len(ap), len(bp))
    acc = None
    for i in range(len(ap)):
        for j in range(len(bp)):
            if i + j < order:
                t = lax.dot_general(ap[i], bp[j], dims, preferred_element_type=F32)
                acc = t if acc is None else acc + t
    return acc


_BNN = (((2,), (1,)), ((0,), (0,)))
_BNT = (((2,), (2,)), ((0,), (0,)))
_BTN = (((1,), (1,)), ((0,), (0,)))


def _sigmoid(x):
    return 0.5 * jnp.tanh(0.5 * x) + 0.5


def _silu(x):
    h = 0.5 * x
    return h * jnp.tanh(h) + h


def _softplus(x):
    return jnp.maximum(x, 0.0) + jnp.log1p(jnp.exp(-jnp.abs(x)))


def _norm_mod(x, nw, sc, sh):
    ms = jnp.mean(x * x, axis=-1, keepdims=True)
    return x * lax.rsqrt(ms + EPS) * (nw * (1.0 + sc)) + sh


def _shift_rows(cur, prev, s):
    rolled = pltpu.roll(cur, s, axis=0)
    prolled = pltpu.roll(prev, s, axis=0)
    rid = lax.broadcasted_iota(jnp.int32, prev.shape, 0)
    top = jnp.where(rid < s, prolled, rolled[0:HALO])
    return jnp.concatenate([top, rolled[HALO:]], axis=0)


def _causal_conv(cur, halo_ref, w, width, first):
    prev = jnp.where(first, 0.0, halo_ref[...])
    halo_ref[...] = cur[cur.shape[0] - HALO:, :]
    out = cur * w[width - 1:width, :]
    for s in range(1, width):
        out = out + _shift_rows(cur, prev, s) * w[width - 1 - s:width - s, :]
    return out


def _run_interleaved(gens, weights):
    live = list(zip(gens, weights))
    while live:
        for item in list(live):
            gen, w = item
            for _ in range(w):
                if next(gen, "done") == "done":
                    live.remove(item)
                    break


def _mod_kernel(c_ref, w_ref, b_ref, o_ref):
    c = c_ref[...]
    o_ref[0] = _mm(_silu(c), w_ref[0], 2, 2) + b_ref[0]


def _modulation(c, ada_w, ada_b):
    depth, d, nd = ada_w.shape
    b = c.shape[0]
    nj = nd // d
    out = pl.pallas_call(
        _mod_kernel,
        grid=(depth, nj),
        in_specs=[
            pl.BlockSpec((b, d), lambda i, j: (0, 0)),
            pl.BlockSpec((1, d, d), lambda i, j: (i, 0, j)),
            pl.BlockSpec((1, 1, d), lambda i, j: (i, 0, j)),
        ],
        out_specs=pl.BlockSpec((1, b, d), lambda i, j: (i, 0, j)),
        out_shape=jax.ShapeDtypeStruct((depth, b, nd), F32),
        name="adaln_mod",
    )(c, ada_w, ada_b.reshape(depth, 1, nd))
    return out.reshape(depth, b, nj, d)


def _mix_kernel(*refs, **static):
    g_id = pl.program_id(0)
    for par in (0, 1):
        pl.when(lax.rem(g_id, 2) == par)(functools.partial(_mix_step, *refs, par=par, **static))


def _mix_step(xa_ref, moda_ref, xc_ref, modc_ref, nw_ref, win_ref, wg_ref, caw_ref, naw_ref, gmat_ref,
              cqw_ref, alog_ref, dtb_ref, ndw_ref, wout_ref,
              y_ref,
              h_scr, halo_a, halo_q, q_scr, k_scr, v_scr, gb_scr, ya_scr, zs_scr,
              s_scr, u_scr, wq_scr, a_scr, kd_scr, cd_scr, o_scr,
              *, par, nchunk, nt, ntiles, aw, dnw, p):
    g_id = pl.program_id(0)
    prev = 1 - par
    first_a = lax.rem(jnp.minimum(g_id, ntiles - 1), nt) == 0
    first_c = lax.rem(g_id - 2, nt) == 0

    if par == 0:
        @pl.when(g_id == 0)
        def _():
            for ref in (halo_a, halo_q, q_scr, k_scr, v_scr, gb_scr, ya_scr, zs_scr,
                        s_scr, u_scr, wq_scr, a_scr, kd_scr, cd_scr, o_scr):
                ref[...] = jnp.zeros_like(ref)

    c_len, dk, nh = CHUNK, DN_HEAD_DIM, DN_HEADS
    n = nchunk * nh
    tm = nchunk * c_len
    ri = lax.broadcasted_iota(jnp.int32, (n, c_len, c_len), 1)
    ci = lax.broadcasted_iota(jnp.int32, (n, c_len, c_len), 2)
    causal = ri >= ci
    strict = ri > ci

    def blk(v, s):
        return lax.shift_right_logical(v, jnp.int32(s.bit_length() - 1))

    def project():
        m = moda_ref[0]
        h = _norm_mod(xa_ref[0], nw_ref[...], m[1:2], m[0:1])
        h_scr[...] = h.astype(BF16)
        hb = h_scr[...]
        base = 3 * aw

        def proj(lo, width):
            return _mm(hb, win_ref[:, lo:lo + width])

        def qkv_part(part, raw, dst):
            cv = _causal_conv(raw, halo_q.at[part], cqw_ref[:, part * dnw:(part + 1) * dnw], 4, first_a)
            s = _silu(cv)
            for hh in range(nh):
                seg = s[:, hh * dk:(hh + 1) * dk]
                if part < 2:
                    n2 = jnp.sum(seg * seg, axis=-1, keepdims=True)
                    seg = seg * lax.rsqrt(n2 + EPS)
                if part == 0:
                    seg = seg * (dk ** -0.5)
                dst[par, :, hh] = seg.reshape(nchunk, c_len, dk)

        a_b, a_c, a_x = proj(0, aw), proj(aw, aw), proj(2 * aw, aw)
        yield
        raw_q = proj(base, dnw)
        ya = a_b * _causal_conv(a_c * a_x, halo_a, caw_ref[...], 3, first_a)
        ss = _mm(ya * ya, gmat_ref[...], 2, 1)
        ya = ya * lax.rsqrt(ss * (1.0 / A_GROUP_DIM) + EPS) * naw_ref[...]
        ya_scr[par] = ya.astype(BF16)
        yield
        raw_k = proj(base + dnw, dnw)
        qkv_part(0, raw_q, q_scr)
        yield
        raw_v = proj(base + 2 * dnw, dnw)
        qkv_part(1, raw_k, k_scr)
        yield
        z = proj(base + 3 * dnw, dnw)
        qkv_part(2, raw_v, v_scr)
        yield
        gates = _mm(hb, wg_ref[...])
        zs_scr[par] = _silu(z).astype(BF16)
        lane = lax.broadcasted_iota(jnp.int32, gates.shape, 1)
        gdec = -jnp.exp(alog_ref[...]) * _softplus(gates + dtb_ref[...])
        gb_scr[par] = jnp.where(lane < nh, _sigmoid(gates), gdec)

    def prepare():
        q, k, v = (ref[prev].reshape(n, c_len, dk) for ref in (q_scr, k_scr, v_scr))
        gbt = gb_scr[prev].reshape(nchunk, c_len, LANES)
        gcum = _mm(causal[:nchunk].astype(BF16), gbt, 1, 3, _BNN)
        gcum_t = _mm(gbt, (ri <= ci)[:nchunk].astype(BF16), 3, 1, _BTN)
        yield
        pick = [(c, hh) for c in range(nchunk) for hh in range(nh)]
        beta = jnp.stack([gbt[c, :, hh:hh + 1] for c, hh in pick])
        gc = jnp.stack([gcum[c, :, nh + hh:nh + hh + 1] for c, hh in pick])
        gcr = jnp.stack([gcum_t[c, nh + hh:nh + hh + 1, :] for c, hh in pick])
        glast = gc[:, c_len - 1:c_len, :]
        decay = jnp.where(causal, jnp.exp(gc - gcr), 0.0)
        eg = jnp.exp(gc)
        kb = k * beta
        qkk = _mm(jnp.concatenate([q, kb], axis=1), k, p, p, _BNT)
        yield
        amat = qkk[:, :c_len] * decay
        lmat = jnp.where(strict, qkk[:, c_len:] * decay, 0.0)
        tinv = (ri == ci).astype(F32) - jnp.where(blk(ri, 2) == blk(ci, 2), lmat, 0.0)
        s = 4
        while s <= c_len:
            e = jnp.where((blk(ri, s) == blk(ci, s)) & (blk(ri, s // 2) != blk(ci, s // 2)), lmat, 0.0)
            et = _mm(e, tinv, p, p, _BNN)
            yield
            tinv = tinv - _mm(tinv, et, p, p, _BNN)
            yield
            s *= 2
        sol = _mm(tinv, jnp.concatenate([v * beta, kb * eg], axis=2), p, p, _BNN)
        yield
        u_scr[par] = sol[:, :, :dk]
        wq_scr[par] = jnp.concatenate([sol[:, :, dk:], q * eg], axis=1)
        a_scr[par] = amat
        kd_scr[par] = k * jnp.exp(glast - gc)
        cd_scr[par] = jnp.broadcast_to(jnp.exp(glast), cd_scr.shape[1:])

    def recur():
        ya_old, zs_old = ya_scr[par], zs_scr[par]
        s = jnp.where(first_c, 0.0, s_scr[...])
        for c in range(nchunk):
            sl = pl.ds(c * nh, nh)
            r = _mm(wq_scr[prev, sl], s, p, p, _BNN)
            yield
            v_new = u_scr[prev, sl] - r[:, :c_len]
            o_scr[c] = r[:, c_len:] + _mm(a_scr[prev, sl], v_new, p, p, _BNN)
            s = s * cd_scr[prev, sl] + _mm(kd_scr[prev, sl], v_new, p, p, _BTN)
            yield
        s_scr[...] = s
        m = modc_ref[0]
        ndw = ndw_ref[...]
        parts = [ya_old]
        for hh in range(nh):
            seg = o_scr[:, hh].reshape(tm, dk)
            ms = jnp.mean(seg * seg, axis=-1, keepdims=True)
            zs = zs_old[:, hh * dk:(hh + 1) * dk].astype(F32)
            parts.append((seg * lax.rsqrt(ms + EPS) * ndw * zs).astype(BF16))
        yield
        y = _mm(jnp.concatenate(parts, axis=1), wout_ref[...])
        yield
        y_ref[0] = xc_ref[0] + m[2:3] * y

    _run_interleaved([recur(), project(), prepare()], [2, 1, 3])


def _mix(x, mod_i, norm1_w, win_main, w_gates, conv_a_w, norm_a_w, gmat, conv_qkv_w, alog_pad, dtb_pad,
         norm_dn_w, wout, *, tm, passes):
    b, l, d = x.shape
    aw = conv_a_w.shape[-1]
    dnw = conv_qkv_w.shape[-1] // 3
    pm = win_main.shape[-1]
    nt = l // tm
    ntiles = b * nt
    nchunk = tm // CHUNK
    n = nchunk * DN_HEADS
    dk = DN_HEAD_DIM

    def tile(g, lag):
        t = jnp.clip(g - lag, 0, ntiles - 1)
        return t // nt, t % nt

    x_spec = lambda lag: pl.BlockSpec((1, tm, d), lambda g: (*tile(g, lag), 0))
    mod_spec = lambda lag: pl.BlockSpec((1, N_MOD, d), lambda g: (tile(g, lag)[0], 0, 0))
    fix2 = lambda g: (0, 0)
    const = lambda a, **kw: pl.BlockSpec(a.shape, fix2, **kw)
    chunked = (nchunk, DN_HEADS, CHUNK, dk)
    scratch = [
        pltpu.VMEM((tm, d), BF16),
        pltpu.VMEM((HALO, aw), F32),
        pltpu.VMEM((3, HALO, dnw), F32),
        pltpu.VMEM((2,) + chunked, F32),
        pltpu.VMEM((2,) + chunked, F32),
        pltpu.VMEM((2,) + chunked, F32),
        pltpu.VMEM((2, tm, LANES), F32),
        pltpu.VMEM((2, tm, aw), BF16),
        pltpu.VMEM((2, tm, dnw), BF16),
        pltpu.VMEM((DN_HEADS, dk, dk), F32),
        pltpu.VMEM((2, n, CHUNK, dk), F32),
        pltpu.VMEM((2, n, 2 * CHUNK, dk), F32),
        pltpu.VMEM((2, n, CHUNK, CHUNK), F32),
        pltpu.VMEM((2, n, CHUNK, dk), F32),
        pltpu.VMEM((2, n, 1, dk), F32),
        pltpu.VMEM(chunked, F32),
    ]
    est = (6 * tm * d * 4 + (pm + LANES + d) * d * 2 + aw * aw * 2 + tm * d * 2 + 6 * tm * dnw * 4
           + 2 * tm * LANES * 4 + 2 * tm * (aw + dnw) * 2 + 2 * n * (4 * CHUNK + 2 * CHUNK + HALO) * dk * 4
           + tm * dnw * 4 + 24 * 1024 * 1024)
    kern = functools.partial(_mix_kernel, nchunk=nchunk, nt=nt, ntiles=ntiles, aw=aw, dnw=dnw, p=passes)
    return pl.pallas_call(
        kern,
        grid=(ntiles + 2,),
        in_specs=[
            x_spec(0), mod_spec(0), x_spec(2), mod_spec(2),
            const(norm1_w),
            const(win_main, pipeline_mode=pl.Buffered(1)),
            const(w_gates), const(conv_a_w), const(norm_a_w), const(gmat), const(conv_qkv_w),
            const(alog_pad), const(dtb_pad), const(norm_dn_w),
            const(wout, pipeline_mode=pl.Buffered(1)),
        ],
        out_specs=x_spec(2),
        out_shape=jax.ShapeDtypeStruct((b, l, d), F32),
        scratch_shapes=scratch,
        compiler_params=pltpu.CompilerParams(
            dimension_semantics=("arbitrary",), vmem_limit_bytes=_vmem_limit(est)),
        name="token_mix",
    )(x, mod_i, x, mod_i, norm1_w, win_main, w_gates, conv_a_w, norm_a_w, gmat, conv_qkv_w, alog_pad, dtb_pad,
      norm_dn_w, wout)


def _ffn_kernel(x_ref, mod_ref, nw_ref, wup_ref, cw_ref, wdn_ref, nf_ref, y_ref,
                h_scr, acc_scr, halo_scr, u_scr, act_scr, *, nblk, final):
    t = pl.program_id(1)

    @pl.when(t == 0)
    def _():
        halo_scr[...] = jnp.zeros_like(halo_scr)

    m = mod_ref[0]
    x = x_ref[0]
    h = _norm_mod(x, nw_ref[...], m[4:5], m[3:4])
    h_scr[...] = h.astype(BF16)
    acc_scr[...] = jnp.zeros_like(acc_scr)
    tm = x.shape[0]

    def up(j, slot):
        u_scr[slot] = _mm(h_scr[...], wup_ref[j])

    def down(j, slot):
        w = cw_ref[j]
        for r in range(0, tm, STRIP):
            prev = halo_scr[j] if r == 0 else u_scr[slot, r - HALO:r]
            ext = jnp.concatenate([prev, u_scr[slot, r:r + STRIP]], axis=0)
            cv = (ext[HALO:] * w[2:3] + ext[HALO - 1:HALO - 1 + STRIP] * w[1:2]
                  + ext[HALO - 2:HALO - 2 + STRIP] * w[0:1])
            act_scr[r:r + STRIP] = (_silu(cv[:, :FF_BLK]) * cv[:, FF_BLK:]).astype(BF16)
        halo_scr[j] = u_scr[slot, tm - HALO:tm]
        acc_scr[...] += _mm(act_scr[...], wdn_ref[j])

    up(0, 0)
    for j in range(nblk):
        if j + 1 < nblk:
            up(j + 1, (j + 1) % 2)
        down(j, j % 2)
    y = x + m[5:6] * acc_scr[...]
    if final:
        ms = jnp.mean(y * y, axis=-1, keepdims=True)
        y = y * lax.rsqrt(ms + EPS) * nf_ref[...]
    y_ref[0] = y


def _ffn(x, mod_i, norm2_w, wup_blk, cw_blk, wdn_blk, norm_f_w, *, tm, final):
    b, l, d = x.shape
    nblk = wup_blk.shape[0]
    nt = l // tm
    row = lambda bi, ti: (bi, ti, 0)
    fix2 = lambda bi, ti: (0, 0)
    fix3 = lambda bi, ti: (0, 0, 0)
    est = (4 * tm * d * 4 + wup_blk.size * 2 + wdn_blk.size * 2 + tm * d * 2 + tm * d * 4
           + nblk * HALO * 2 * FF_BLK * 4 + 10 * tm * 2 * FF_BLK * 4 + 4 * 1024 * 1024)
    kern = functools.partial(_ffn_kernel, nblk=nblk, final=final)
    return pl.pallas_call(
        kern,
        grid=(b, nt),
        in_specs=[
            pl.BlockSpec((1, tm, d), row),
            pl.BlockSpec((1, N_MOD, d), lambda bi, ti: (bi, 0, 0)),
            pl.BlockSpec((1, d), fix2),
            pl.BlockSpec(wup_blk.shape, fix3, pipeline_mode=pl.Buffered(1)),
            pl.BlockSpec(cw_blk.shape, fix3),
            pl.BlockSpec(wdn_blk.shape, fix3, pipeline_mode=pl.Buffered(1)),
            pl.BlockSpec((1, d), fix2),
        ],
        out_specs=pl.BlockSpec((1, tm, d), row),
        out_shape=jax.ShapeDtypeStruct((b, l, d), F32),
        scratch_shapes=[
            pltpu.VMEM((tm, d), BF16),
            pltpu.VMEM((tm, d), F32),
            pltpu.VMEM((nblk, HALO, 2 * FF_BLK), F32),
            pltpu.VMEM((2, tm, 2 * FF_BLK), F32),
            pltpu.VMEM((tm, FF_BLK), BF16),
        ],
        compiler_params=pltpu.CompilerParams(
            dimension_semantics=("arbitrary", "arbitrary"), vmem_limit_bytes=_vmem_limit(est)),
        name="ffn",
    )(x, mod_i, norm2_w, wup_blk, cw_blk, wdn_blk, norm_f_w)


def _pad_rows(w, rows):
    return jnp.pad(w, ((0, rows - w.shape[0]), (0, 0)))


def kernel(x, c, ada_w, ada_b, norm1_w, w_in, conv_a_w, norm_a_w, conv_qkv_w, a_log, dt_bias, norm_dn_w,
           w_out, norm2_w, w_up, conv_ff_w, w_down, norm_f_w):
    b, l, d = x.shape
    depth = ada_w.shape[0]
    aw = conv_a_w.shape[-1]
    dnw = conv_qkv_w.shape[-1] // 3
    d_ff = w_down.shape[1]
    pm = 3 * aw + 4 * dnw
    nblk = d_ff // FF_BLK
    assert d_ff % FF_BLK == 0 and dnw == DN_HEADS * DN_HEAD_DIM and w_in.shape[-1] == pm + 2 * DN_HEADS
    tm = min(512, l)
    assert l % tm == 0 and tm % CHUNK == 0 and tm % STRIP == 0

    mod = _modulation(c, ada_w, ada_b)

    gid = jnp.arange(aw) // A_GROUP_DIM
    gmat = (gid[:, None] == gid[None, :]).astype(BF16)

    for i in range(depth):
        win_main = w_in[i, :, :pm].astype(BF16)
        w_gates = jnp.pad(w_in[i, :, pm:], ((0, 0), (0, LANES - 2 * DN_HEADS))).astype(BF16)
        alog_pad = jnp.pad(a_log[i][None, :], ((0, 0), (DN_HEADS, LANES - 2 * DN_HEADS)))
        dtb_pad = jnp.pad(dt_bias[i][None, :], ((0, 0), (DN_HEADS, LANES - 2 * DN_HEADS)))
        x = _mix(x, mod[i], norm1_w[i][None, :], win_main, w_gates, _pad_rows(conv_a_w[i], HALO),
                 norm_a_w[i][None, :], gmat, _pad_rows(conv_qkv_w[i], HALO), alog_pad, dtb_pad,
                 norm_dn_w[i][None, :], w_out[i].astype(BF16), tm=tm, passes=1)

        wu = w_up[i]
        wup_blk = jnp.concatenate(
            [wu[:, :d_ff].reshape(d, nblk, FF_BLK), wu[:, d_ff:].reshape(d, nblk, FF_BLK)], axis=-1)
        wup_blk = jnp.transpose(wup_blk, (1, 0, 2)).astype(BF16)
        cw = conv_ff_w[i]
        cw_blk = jnp.concatenate(
            [cw[:, :d_ff].reshape(-1, nblk, FF_BLK), cw[:, d_ff:].reshape(-1, nblk, FF_BLK)], axis=-1)
        cw_blk = jnp.pad(jnp.transpose(cw_blk, (1, 0, 2)), ((0, 0), (0, HALO - cw.shape[0]), (0, 0)))
        wdn_blk = w_down[i].reshape(nblk, FF_BLK, d).astype(BF16)
        x = _ffn(x, mod[i], norm2_w[i][None, :], wup_blk, cw_blk, wdn_blk, norm_f_w[None, :],
                 tm=tm, final=(i == depth - 1))
    return x
```

```python
import functools

import jax
import jax.numpy as jnp
from jax import lax
from jax.experimental import pallas as pl
from jax.experimental.pallas import tpu as pltpu

F32 = jnp.float32
BF16 = jnp.bfloat16
EPS = 1e-6

A_GROUP_DIM = 64
DN_HEADS = 4
DN_HEAD_DIM = 128
CHUNK = 64
N_MOD = 6
HALO = 8
LANES = 128
FF_BLK = 256
STRIP = 32
VMEM_CAP = 60 * 1024 * 1024


def _vmem_limit(nbytes):
    return int(min(VMEM_CAP, nbytes))


def _parts(a, n):
    out = []
    r = a
    for i in range(n):
        p = r.astype(BF16)
        out.append(p)
        if i + 1 < n:
            r = r - p.astype(F32)
    return out


def _mm(a, b, pa=1, pb=1, dims=(((1,), (0,)), ((), ()))):
    ap = _parts(a, pa) if a.dtype != BF16 else [a]
    bp = _parts(b, pb) if b.dtype != BF16 else [b]
    order = max(len(ap), len(bp))
    acc = None
    for i in range(len(ap)):
        for j in range(len(bp)):
            if i + j < order:
                t = lax.dot_general(ap[i], bp[j], dims, preferred_element_type=F32)
                acc = t if acc is None else acc + t
    return acc


_BNN = (((2,), (1,)), ((0,), (0,)))
_BNT = (((2,), (2,)), ((0,), (0,)))
_BTN = (((1,), (1,)), ((0,), (0,)))


def _sigmoid(x):
    return 0.5 * jnp.tanh(0.5 * x) + 0.5


def _silu(x):
    h = 0.5 * x
    return h * jnp.tanh(h) + h


def _softplus(x):
    return jnp.maximum(x, 0.0) + jnp.log1p(jnp.exp(-jnp.abs(x)))


def _norm_mod(x, nw, sc, sh):
    ms = jnp.mean(x * x, axis=-1, keepdims=True)
    return x * lax.rsqrt(ms + EPS) * (nw * (1.0 + sc)) + sh


def _shift_rows(cur, prev, s):
    rolled = pltpu.roll(cur, s, axis=0)
    prolled = pltpu.roll(prev, s, axis=0)
    rid = lax.broadcasted_iota(jnp.int32, prev.shape, 0)
    top = jnp.where(rid < s, prolled, rolled[0:HALO])
    return jnp.concatenate([top, rolled[HALO:]], axis=0)


def _causal_conv(cur, halo_ref, w, width, first):
    prev = jnp.where(first, 0.0, halo_ref[...])
    halo_ref[...] = cur[cur.shape[0] - HALO:, :]
    out = cur * w[width - 1:width, :]
    for s in range(1, width):
        out = out + _shift_rows(cur, prev, s) * w[width - 1 - s:width - s, :]
    return out


def _run_interleaved(gens, weights):
    live = list(zip(gens, weights))
    while live:
        for item in list(live):
            gen, w = item
            for _ in range(w):
                if next(gen, "done") == "done":
                    live.remove(item)
                    break


def _mod_kernel(c_ref, w_ref, b_ref, o_ref):
    c = c_ref[...]
    o_ref[0] = _mm(_silu(c), w_ref[0], 2, 2) + b_ref[0]


def _modulation(c, ada_w, ada_b):
    depth, d, nd = ada_w.shape
    b = c.shape[0]
    nj = nd // d
    out = pl.pallas_call(
        _mod_kernel,
        grid=(depth, nj),
        in_specs=[
            pl.BlockSpec((b, d), lambda i, j: (0, 0)),
            pl.BlockSpec((1, d, d), lambda i, j: (i, 0, j)),
            pl.BlockSpec((1, 1, d), lambda i, j: (i, 0, j)),
        ],
        out_specs=pl.BlockSpec((1, b, d), lambda i, j: (i, 0, j)),
        out_shape=jax.ShapeDtypeStruct((depth, b, nd), F32),
        name="adaln_mod",
    )(c, ada_w, ada_b.reshape(depth, 1, nd))
    return out.reshape(depth, b, nj, d)


def _mix_kernel(*refs, **static):
    g_id = pl.program_id(0)
    for par in (0, 1):
        pl.when(lax.rem(g_id, 2) == par)(functools.partial(_mix_step, *refs, par=par, **static))


def _mix_step(xa_ref, moda_ref, xc_ref, modc_ref, nw_ref, win_ref, wg_ref, caw_ref, naw_ref, gmat_ref,
              cqw_ref, alog_ref, dtb_ref, ndw_ref, wout_ref,
              y_ref,
              h_scr, halo_a, halo_q, q_scr, k_scr, v_scr, gb_scr, ya_scr, zs_scr,
              s_scr, u_scr, wq_scr, a_scr, kd_scr, cd_scr, o_scr,
              *, par, nchunk, nt, ntiles, aw, dnw, p):
    g_id = pl.program_id(0)
    prev = 1 - par
    first_a = lax.rem(jnp.minimum(g_id, ntiles - 1), nt) == 0
    first_c = lax.rem(g_id - 2, nt) == 0

    if par == 0:
        @pl.when(g_id == 0)
        def _():
            for ref in (halo_a, halo_q, q_scr, k_scr, v_scr, gb_scr, ya_scr, zs_scr,
                        s_scr, u_scr, wq_scr, a_scr, kd_scr, cd_scr, o_scr):
                ref[...] = jnp.zeros_like(ref)

    c_len, dk, nh = CHUNK, DN_HEAD_DIM, DN_HEADS
    n = nchunk * nh
    tm = nchunk * c_len
    ri = lax.broadcasted_iota(jnp.int32, (n, c_len, c_len), 1)
    ci = lax.broadcasted_iota(jnp.int32, (n, c_len, c_len), 2)
    causal = ri >= ci
    strict = ri > ci

    def blk(v, s):
        return lax.shift_right_logical(v, jnp.int32(s.bit_length() - 1))

    def project():
        m = moda_ref[0]
        h = _norm_mod(xa_ref[0], nw_ref[...], m[1:2], m[0:1])
        h_scr[...] = h.astype(BF16)
        hb = h_scr[...]
        base = 3 * aw

        def proj(lo, width):
            return _mm(hb, win_ref[:, lo:lo + width])

        def qkv_part(part, raw, dst):
            cv = _causal_conv(raw, halo_q.at[part], cqw_ref[:, part * dnw:(part + 1) * dnw], 4, first_a)
            s = _silu(cv)
            for hh in range(nh):
                seg = s[:, hh * dk:(hh + 1) * dk]
                if part < 2:
                    n2 = jnp.sum(seg * seg, axis=-1, keepdims=True)
                    seg = seg * lax.rsqrt(n2 + EPS)
                if part == 0:
                    seg = seg * (dk ** -0.5)
                dst[par, :, hh] = seg.reshape(nchunk, c_len, dk)

        a_b, a_c, a_x = proj(0, aw), proj(aw, aw), proj(2 * aw, aw)
        yield
        raw_q = proj(base, dnw)
        yield
        ya = a_b * _causal_conv(a_c * a_x, halo_a, caw_ref[...], 3, first_a)
        ss = _mm(ya * ya, gmat_ref[...], 2, 1)
        ya = ya * lax.rsqrt(ss * (1.0 / A_GROUP_DIM) + EPS) * naw_ref[...]
        ya_scr[par] = ya.astype(BF16)
        yield
        raw_k = proj(base + dnw, dnw)
        yield
        qkv_part(0, raw_q, q_scr)
        yield
        raw_v = proj(base + 2 * dnw, dnw)
        yield
        qkv_part(1, raw_k, k_scr)
        yield
        z = proj(base + 3 * dnw, dnw)
        yield
        qkv_part(2, raw_v, v_scr)
        yield
        gates = _mm(hb, wg_ref[...])
        zs_scr[par] = _silu(z).astype(BF16)
        lane = lax.broadcasted_iota(jnp.int32, gates.shape, 1)
        gdec = -jnp.exp(alog_ref[...]) * _softplus(gates + dtb_ref[...])
        gb_scr[par] = jnp.where(lane < nh, _sigmoid(gates), gdec)

    def prepare():
        q, k, v = (ref[prev].reshape(n, c_len, dk) for ref in (q_scr, k_scr, v_scr))
        gbt = gb_scr[prev].reshape(nchunk, c_len, LANES)
        gcum = _mm(causal[:nchunk].astype(BF16), gbt, 1, 3, _BNN)
        gcum_t = _mm(gbt, (ri <= ci)[:nchunk].astype(BF16), 3, 1, _BTN)
        yield
        pick = [(c, hh) for c in range(nchunk) for hh in range(nh)]
        beta = jnp.stack([gbt[c, :, hh:hh + 1] for c, hh in pick])
        gc = jnp.stack([gcum[c, :, nh + hh:nh + hh + 1] for c, hh in pick])
        gcr = jnp.stack([gcum_t[c, nh + hh:nh + hh + 1, :] for c, hh in pick])
        glast = gc[:, c_len - 1:c_len, :]
        decay = jnp.where(causal, jnp.exp(gc - gcr), 0.0)
        eg = jnp.exp(gc)
        kb = k * beta
        qkk = _mm(jnp.concatenate([q, kb], axis=1), k, p, p, _BNT)
        yield
        amat = qkk[:, :c_len] * decay
        lmat = jnp.where(strict, qkk[:, c_len:] * decay, 0.0)
        tinv = (ri == ci).astype(F32) - jnp.where(blk(ri, 2) == blk(ci, 2), lmat, 0.0)
        s = 4
        while s <= c_len:
            e = jnp.where((blk(ri, s) == blk(ci, s)) & (blk(ri, s // 2) != blk(ci, s // 2)), lmat, 0.0)
            et = _mm(e, tinv, p, p, _BNN)
            yield
            tinv = tinv - _mm(tinv, et, p, p, _BNN)
            yield
            s *= 2
        sol = _mm(tinv, jnp.concatenate([v * beta, kb * eg], axis=2), p, p, _BNN)
        yield
        u_scr[par] = sol[:, :, :dk]
        wq_scr[par] = jnp.concatenate([sol[:, :, dk:], q * eg], axis=1)
        a_scr[par] = amat
        kd_scr[par] = k * jnp.exp(glast - gc)
        cd_scr[par] = jnp.broadcast_to(jnp.exp(glast), cd_scr.shape[1:])

    def recur():
        ya_old, zs_old = ya_scr[par], zs_scr[par]
        s = jnp.where(first_c, 0.0, s_scr[...])
        for c in range(nchunk):
            sl = pl.ds(c * nh, nh)
            r = _mm(wq_scr[prev, sl], s, p, p, _BNN)
            yield
            v_new = u_scr[prev, sl] - r[:, :c_len]
            o_scr[c] = r[:, c_len:] + _mm(a_scr[prev, sl], v_new, p, p, _BNN)
            s = s * cd_scr[prev, sl] + _mm(kd_scr[prev, sl], v_new, p, p, _BTN)
            yield
        s_scr[...] = s
        m = modc_ref[0]
        ndw = ndw_ref[...]
        parts = [ya_old]
        for hh in range(nh):
            seg = o_scr[:, hh].reshape(tm, dk)
            ms = jnp.mean(seg * seg, axis=-1, keepdims=True)
            zs = zs_old[:, hh * dk:(hh + 1) * dk].astype(F32)
            parts.append((seg * lax.rsqrt(ms + EPS) * ndw * zs).astype(BF16))
        yield
        y = _mm(jnp.concatenate(parts, axis=1), wout_ref[...])
        yield
        y_ref[0] = xc_ref[0] + m[2:3] * y

    _run_interleaved([recur(), project(), prepare()], [2, 1, 2])


def _mix(x, mod_i, norm1_w, win_main, w_gates, conv_a_w, norm_a_w, gmat, conv_qkv_w, alog_pad, dtb_pad,
         norm_dn_w, wout, *, tm, passes):
    b, l, d = x.shape
    aw = conv_a_w.shape[-1]
    dnw = conv_qkv_w.shape[-1] // 3
    pm = win_main.shape[-1]
    nt = l // tm
    ntiles = b * nt
    nchunk = tm // CHUNK
    n = nchunk * DN_HEADS
    dk = DN_HEAD_DIM

    def tile(g, lag):
        t = jnp.clip(g - lag, 0, ntiles - 1)
        return t // nt, t % nt

    x_spec = lambda lag: pl.BlockSpec((1, tm, d), lambda g: (*tile(g, lag), 0))
    mod_spec = lambda lag: pl.BlockSpec((1, N_MOD, d), lambda g: (tile(g, lag)[0], 0, 0))
    fix2 = lambda g: (0, 0)
    const = lambda a, **kw: pl.BlockSpec(a.shape, fix2, **kw)
    chunked = (nchunk, DN_HEADS, CHUNK, dk)
    scratch = [
        pltpu.VMEM((tm, d), BF16),
        pltpu.VMEM((HALO, aw), F32),
        pltpu.VMEM((3, HALO, dnw), F32),
        pltpu.VMEM((2,) + chunked, F32),
        pltpu.VMEM((2,) + chunked, F32),
        pltpu.VMEM((2,) + chunked, F32),
        pltpu.VMEM((2, tm, LANES), F32),
        pltpu.VMEM((2, tm, aw), BF16),
        pltpu.VMEM((2, tm, dnw), BF16),
        pltpu.VMEM((DN_HEADS, dk, dk), F32),
        pltpu.VMEM((2, n, CHUNK, dk), F32),
        pltpu.VMEM((2, n, 2 * CHUNK, dk), F32),
        pltpu.VMEM((2, n, CHUNK, CHUNK), F32),
        pltpu.VMEM((2, n, CHUNK, dk), F32),
        pltpu.VMEM((2, n, 1, dk), F32),
        pltpu.VMEM(chunked, F32),
    ]
    est = (6 * tm * d * 4 + (pm + LANES + d) * d * 2 + aw * aw * 2 + tm * d * 2 + 6 * tm * dnw * 4
           + 2 * tm * LANES * 4 + 2 * tm * (aw + dnw) * 2 + 2 * n * (4 * CHUNK + 2 * CHUNK + HALO) * dk * 4
           + tm * dnw * 4 + 24 * 1024 * 1024)
    kern = functools.partial(_mix_kernel, nchunk=nchunk, nt=nt, ntiles=ntiles, aw=aw, dnw=dnw, p=passes)
    return pl.pallas_call(
        kern,
        grid=(ntiles + 2,),
        in_specs=[
            x_spec(0), mod_spec(0), x_spec(2), mod_spec(2),
            const(norm1_w),
            const(win_main, pipeline_mode=pl.Buffered(1)),
            const(w_gates), const(conv_a_w), const(norm_a_w), const(gmat), const(conv_qkv_w),
            const(alog_pad), const(dtb_pad), const(norm_dn_w),
            const(wout, pipeline_mode=pl.Buffered(1)),
        ],
        out_specs=x_spec(2),
        out_shape=jax.ShapeDtypeStruct((b, l, d), F32),
        scratch_shapes=scratch,
        compiler_params=pltpu.CompilerParams(
            dimension_semantics=("arbitrary",), vmem_limit_bytes=_vmem_limit(est)),
        name="token_mix",
    )(x, mod_i, x, mod_i, norm1_w, win_main, w_gates, conv_a_w, norm_a_w, gmat, conv_qkv_w, alog_pad, dtb_pad,
      norm_dn_w, wout)


def _ffn_kernel(x_ref, mod_ref, nw_ref, wup_ref, cw_ref, wdn_ref, nf_ref, y_ref,
                h_scr, acc_scr, halo_scr, u_scr, act_scr, *, nblk, final):
    t = pl.program_id(1)

    @pl.when(t == 0)
    def _():
        halo_scr[...] = jnp.zeros_like(halo_scr)

    m = mod_ref[0]
    x = x_ref[0]
    h = _norm_mod(x, nw_ref[...], m[4:5], m[3:4])
    h_scr[...] = h.astype(BF16)
    acc_scr[...] = jnp.zeros_like(acc_scr)
    tm = x.shape[0]

    def up(j, slot):
        u_scr[slot] = _mm(h_scr[...], wup_ref[j])

    def down(j, slot):
        w = cw_ref[j]
        half = j % 2
        for r in range(0, tm, STRIP):
            prev = halo_scr[j] if r == 0 else u_scr[slot, r - HALO:r]
            ext = jnp.concatenate([prev, u_scr[slot, r:r + STRIP]], axis=0)
            cv = (ext[HALO:] * w[2:3] + ext[HALO - 1:HALO - 1 + STRIP] * w[1:2]
                  + ext[HALO - 2:HALO - 2 + STRIP] * w[0:1])
            act = (_silu(cv[:, :FF_BLK]) * cv[:, FF_BLK:]).astype(BF16)
            act_scr[r:r + STRIP, half * FF_BLK:(half + 1) * FF_BLK] = act
        halo_scr[j] = u_scr[slot, tm - HALO:tm]
        if half == 1:
            acc_scr[...] += _mm(act_scr[...], wdn_ref[j - 1:j + 1].reshape(2 * FF_BLK, -1))
        elif j == nblk - 1:
            acc_scr[...] += _mm(act_scr[:, :FF_BLK], wdn_ref[j])

    up(0, 0)
    for j in range(nblk):
        if j + 1 < nblk:
            up(j + 1, (j + 1) % 2)
        down(j, j % 2)
    y = x + m[5:6] * acc_scr[...]
    if final:
        ms = jnp.mean(y * y, axis=-1, keepdims=True)
        y = y * lax.rsqrt(ms + EPS) * nf_ref[...]
    y_ref[0] = y


def _ffn(x, mod_i, norm2_w, wup_blk, cw_blk, wdn_blk, norm_f_w, *, tm, final):
    b, l, d = x.shape
    nblk = wup_blk.shape[0]
    nt = l // tm
    row = lambda bi, ti: (bi, ti, 0)
    fix2 = lambda bi, ti: (0, 0)
    fix3 = lambda bi, ti: (0, 0, 0)
    est = (4 * tm * d * 4 + wup_blk.size * 2 + wdn_blk.size * 2 + tm * d * 2 + tm * d * 4
           + nblk * HALO * 2 * FF_BLK * 4 + 10 * tm * 2 * FF_BLK * 4 + 4 * 1024 * 1024)
    kern = functools.partial(_ffn_kernel, nblk=nblk, final=final)
    return pl.pallas_call(
        kern,
        grid=(b, nt),
        in_specs=[
            pl.BlockSpec((1, tm, d), row),
            pl.BlockSpec((1, N_MOD, d), lambda bi, ti: (bi, 0, 0)),
            pl.BlockSpec((1, d), fix2),
            pl.BlockSpec(wup_blk.shape, fix3, pipeline_mode=pl.Buffered(1)),
            pl.BlockSpec(cw_blk.shape, fix3),
            pl.BlockSpec(wdn_blk.shape, fix3, pipeline_mode=pl.Buffered(1)),
            pl.BlockSpec((1, d), fix2),
        ],
        out_specs=pl.BlockSpec((1, tm, d), row),
        out_shape=jax.ShapeDtypeStruct((b, l, d), F32),
        scratch_shapes=[
            pltpu.VMEM((tm, d), BF16),
            pltpu.VMEM((tm, d), F32),
            pltpu.VMEM((nblk, HALO, 2 * FF_BLK), F32),
            pltpu.VMEM((2, tm, 2 * FF_BLK), F32),
            pltpu.VMEM((tm, 2 * FF_BLK), BF16),
        ],
        compiler_params=pltpu.CompilerParams(
            dimension_semantics=("arbitrary", "arbitrary"), vmem_limit_bytes=_vmem_limit(est)),
        name="ffn",
    )(x, mod_i, norm2_w, wup_blk, cw_blk, wdn_blk, norm_f_w)


def _pad_rows(w, rows):
    return jnp.pad(w, ((0, rows - w.shape[0]), (0, 0)))


def kernel(x, c, ada_w, ada_b, norm1_w, w_in, conv_a_w, norm_a_w, conv_qkv_w, a_log, dt_bias, norm_dn_w,
           w_out, norm2_w, w_up, conv_ff_w, w_down, norm_f_w):
    b, l, d = x.shape
    depth = ada_w.shape[0]
    aw = conv_a_w.shape[-1]
    dnw = conv_qkv_w.shape[-1] // 3
    d_ff = w_down.shape[1]
    pm = 3 * aw + 4 * dnw
    nblk = d_ff // FF_BLK
    assert d_ff % FF_BLK == 0 and dnw == DN_HEADS * DN_HEAD_DIM and w_in.shape[-1] == pm + 2 * DN_HEADS
    tm = min(512, l)
    tm_ffn = min(256, l)
    assert l % tm == 0 and tm % CHUNK == 0 and l % tm_ffn == 0 and tm_ffn % STRIP == 0

    mod = _modulation(c, ada_w, ada_b)

    gid = jnp.arange(aw) // A_GROUP_DIM
    gmat = (gid[:, None] == gid[None, :]).astype(BF16)

    for i in range(depth):
        win_main = w_in[i, :, :pm].astype(BF16)
        w_gates = jnp.pad(w_in[i, :, pm:], ((0, 0), (0, LANES - 2 * DN_HEADS))).astype(BF16)
        alog_pad = jnp.pad(a_log[i][None, :], ((0, 0), (DN_HEADS, LANES - 2 * DN_HEADS)))
        dtb_pad = jnp.pad(dt_bias[i][None, :], ((0, 0), (DN_HEADS, LANES - 2 * DN_HEADS)))
        x = _mix(x, mod[i], norm1_w[i][None, :], win_main, w_gates, _pad_rows(conv_a_w[i], HALO),
                 norm_a_w[i][None, :], gmat, _pad_rows(conv_qkv_w[i], HALO), alog_pad, dtb_pad,
                 norm_dn_w[i][None, :], w_out[i].astype(BF16), tm=tm, passes=1)

        wu = w_up[i]
        wup_blk = jnp.concatenate(
            [wu[:, :d_ff].reshape(d, nblk, FF_BLK), wu[:, d_ff:].reshape(d, nblk, FF_BLK)], axis=-1)
        wup_blk = jnp.transpose(wup_blk, (1, 0, 2)).astype(BF16)
        cw = conv_ff_w[i]
        cw_blk = jnp.concatenate(
            [cw[:, :d_ff].reshape(-1, nblk, FF_BLK), cw[:, d_ff:].reshape(-1, nblk, FF_BLK)], axis=-1)
        cw_blk = jnp.pad(jnp.transpose(cw_blk, (1, 0, 2)), ((0, 0), (0, HALO - cw.shape[0]), (0, 0)))
        wdn_blk = w_down[i].reshape(nblk, FF_BLK, d).astype(BF16)
        x = _ffn(x, mod[i], norm2_w[i][None, :], wup_blk, cw_blk, wdn_blk, norm_f_w[None, :],
                 tm=tm_ffn, final=(i == depth - 1))
    return x
```

```python
import functools

import jax
import jax.numpy as jnp
from jax import lax
from jax.experimental import pallas as pl
from jax.experimental.pallas import tpu as pltpu

F32 = jnp.float32
BF16 = jnp.bfloat16
EPS = 1e-6

A_GROUP_DIM = 64
DN_HEADS = 4
DN_HEAD_DIM = 128
CHUNK = 64
N_MOD = 6
HALO = 8
LANES = 128
FF_BLK = 256
STRIP = 32
VMEM_CAP = 60 * 1024 * 1024


def _vmem_limit(nbytes):
    return int(min(VMEM_CAP, nbytes))


def _parts(a, n):
    out = []
    r = a
    for i in range(n):
        p = r.astype(BF16)
        out.append(p)
        if i + 1 < n:
            r = r - p.astype(F32)
    return out


def _mm(a, b, pa=1, pb=1, dims=(((1,), (0,)), ((), ()))):
    ap = _parts(a, pa) if a.dtype != BF16 else [a]
    bp = _parts(b, pb) if b.dtype != BF16 else [b]
    order = max(len(ap), len(bp))
    acc = None
    for i in range(len(ap)):
        for j in range(len(bp)):
            if i + j < order:
                t = lax.dot_general(ap[i], bp[j], dims, preferred_element_type=F32)
                acc = t if acc is None else acc + t
    return acc


_BNN = (((2,), (1,)), ((0,), (0,)))
_BNT = (((2,), (2,)), ((0,), (0,)))
_BTN = (((1,), (1,)), ((0,), (0,)))


def _sigmoid(x):
    return 0.5 * jnp.tanh(0.5 * x) + 0.5


def _silu(x):
    h = 0.5 * x
    return h * jnp.tanh(h) + h


def _softplus(x):
    return jnp.maximum(x, 0.0) + jnp.log1p(jnp.exp(-jnp.abs(x)))


def _norm_mod(x, nw, sc, sh):
    ms = jnp.mean(x * x, axis=-1, keepdims=True)
    return x * lax.rsqrt(ms + EPS) * (nw * (1.0 + sc)) + sh


def _shift_rows(cur, prev, s):
    rolled = pltpu.roll(cur, s, axis=0)
    prolled = pltpu.roll(prev, s, axis=0)
    rid = lax.broadcasted_iota(jnp.int32, prev.shape, 0)
    top = jnp.where(rid < s, prolled, rolled[0:HALO])
    return jnp.concatenate([top, rolled[HALO:]], axis=0)


def _causal_conv(cur, halo_ref, w, width, first):
    prev = jnp.where(first, 0.0, halo_ref[...])
    halo_ref[...] = cur[cur.shape[0] - HALO:, :]
    out = cur * w[width - 1:width, :]
    for s in range(1, width):
        out = out + _shift_rows(cur, prev, s) * w[width - 1 - s:width - s, :]
    return out


def _run_interleaved(gens, weights):
    live = list(zip(gens, weights))
    while live:
        for item in list(live):
            gen, w = item
            for _ in range(w):
                if next(gen, "done") == "done":
                    live.remove(item)
                    break


def _mod_kernel(c_ref, w_ref, b_ref, o_ref):
    c = c_ref[...]
    o_ref[0] = _mm(_silu(c), w_ref[0], 2, 2) + b_ref[0]


def _modulation(c, ada_w, ada_b):
    depth, d, nd = ada_w.shape
    b = c.shape[0]
    nj = nd // d
    out = pl.pallas_call(
        _mod_kernel,
        grid=(depth, nj),
        in_specs=[
            pl.BlockSpec((b, d), lambda i, j: (0, 0)),
            pl.BlockSpec((1, d, d), lambda i, j: (i, 0, j)),
            pl.BlockSpec((1, 1, d), lambda i, j: (i, 0, j)),
        ],
        out_specs=pl.BlockSpec((1, b, d), lambda i, j: (i, 0, j)),
        out_shape=jax.ShapeDtypeStruct((depth, b, nd), F32),
        name="adaln_mod",
    )(c, ada_w, ada_b.reshape(depth, 1, nd))
    return out.reshape(depth, b, nj, d)


def _mix_kernel(*refs, **static):
    g_id = pl.program_id(0)
    for par in (0, 1):
        pl.when(lax.rem(g_id, 2) == par)(functools.partial(_mix_step, *refs, par=par, **static))


def _mix_step(xa_ref, moda_ref, xc_ref, modc_ref, nw_ref, win_ref, wg_ref, caw_ref, naw_ref, gmat_ref,
              cqw_ref, alog_ref, dtb_ref, ndw_ref, wout_ref,
              y_ref,
              h_scr, halo_a, halo_q, q_scr, k_scr, v_scr, gb_scr, ya_scr, zs_scr,
              s_scr, u_scr, wq_scr, a_scr, kd_scr, cd_scr, o_scr,
              *, par, nchunk, nt, ntiles, aw, dnw, p):
    g_id = pl.program_id(0)
    prev = 1 - par
    first_a = lax.rem(jnp.minimum(g_id, ntiles - 1), nt) == 0
    first_c = lax.rem(g_id - 2, nt) == 0

    if par == 0:
        @pl.when(g_id == 0)
        def _():
            for ref in (halo_a, halo_q, q_scr, k_scr, v_scr, gb_scr, ya_scr, zs_scr,
                        s_scr, u_scr, wq_scr, a_scr, kd_scr, cd_scr, o_scr):
                ref[...] = jnp.zeros_like(ref)

    c_len, dk, nh = CHUNK, DN_HEAD_DIM, DN_HEADS
    n = nchunk * nh
    tm = nchunk * c_len
    ri = lax.broadcasted_iota(jnp.int32, (n, c_len, c_len), 1)
    ci = lax.broadcasted_iota(jnp.int32, (n, c_len, c_len), 2)
    causal = ri >= ci
    strict = ri > ci

    def blk(v, s):
        return lax.shift_right_logical(v, jnp.int32(s.bit_length() - 1))

    def project():
        m = moda_ref[0]
        h = _norm_mod(xa_ref[0], nw_ref[...], m[1:2], m[0:1])
        h_scr[...] = h.astype(BF16)
        hb = h_scr[...]
        base = 3 * aw

        def proj(lo, width):
            return _mm(hb, win_ref[:, lo:lo + width])

        def qkv_part(part, raw, dst):
            cv = _causal_conv(raw, halo_q.at[part], cqw_ref[:, part * dnw:(part + 1) * dnw], 4, first_a)
            s = _silu(cv)
            for hh in range(nh):
                seg = s[:, hh * dk:(hh + 1) * dk]
                if part < 2:
                    n2 = jnp.sum(seg * seg, axis=-1, keepdims=True)
                    seg = seg * lax.rsqrt(n2 + EPS)
                if part == 0:
                    seg = seg * (dk ** -0.5)
                dst[par, :, hh] = seg.reshape(nchunk, c_len, dk)

        a_b, a_c, a_x = proj(0, aw), proj(aw, aw), proj(2 * aw, aw)
        yield
        raw_q = proj(base, dnw)
        yield
        ya = a_b * _causal_conv(a_c * a_x, halo_a, caw_ref[...], 3, first_a)
        ss = _mm(ya * ya, gmat_ref[...], 2, 1)
        ya = ya * lax.rsqrt(ss * (1.0 / A_GROUP_DIM) + EPS) * naw_ref[...]
        ya_scr[par] = ya.astype(BF16)
        yield
        raw_k = proj(base + dnw, dnw)
        yield
        qkv_part(0, raw_q, q_scr)
        yield
        raw_v = proj(base + 2 * dnw, dnw)
        yield
        qkv_part(1, raw_k, k_scr)
        yield
        z = proj(base + 3 * dnw, dnw)
        yield
        qkv_part(2, raw_v, v_scr)
        yield
        gates = _mm(hb, wg_ref[...])
        zs_scr[par] = _silu(z).astype(BF16)
        lane = lax.broadcasted_iota(jnp.int32, gates.shape, 1)
        gdec = -jnp.exp(alog_ref[...]) * _softplus(gates + dtb_ref[...])
        gb_scr[par] = jnp.where(lane < nh, _sigmoid(gates), gdec)

    def prepare():
        q, k, v = (ref[prev].reshape(n, c_len, dk) for ref in (q_scr, k_scr, v_scr))
        gbt = gb_scr[prev].reshape(nchunk, c_len, LANES)
        gcum = _mm(causal[:nchunk].astype(BF16), gbt, 1, 3, _BNN)
        gcum_t = _mm(gbt, (ri <= ci)[:nchunk].astype(BF16), 3, 1, _BTN)
        yield
        pick = [(c, hh) for c in range(nchunk) for hh in range(nh)]
        beta = jnp.stack([gbt[c, :, hh:hh + 1] for c, hh in pick])
        gc = jnp.stack([gcum[c, :, nh + hh:nh + hh + 1] for c, hh in pick])
        gcr = jnp.stack([gcum_t[c, nh + hh:nh + hh + 1, :] for c, hh in pick])
        glast = gc[:, c_len - 1:c_len, :]
        decay = jnp.where(causal, jnp.exp(gc - gcr), 0.0)
        eg = jnp.exp(gc)
        kb = k * beta
        qkk = _mm(jnp.concatenate([q, kb], axis=1), k, p, p, _BNT)
        yield
        amat = qkk[:, :c_len] * decay
        lmat = jnp.where(strict, qkk[:, c_len:] * decay, 0.0)
        tinv = (ri == ci).astype(F32) - jnp.where(blk(ri, 2) == blk(ci, 2), lmat, 0.0)
        s = 4
        while s <= c_len:
            e = jnp.where((blk(ri, s) == blk(ci, s)) & (blk(ri, s // 2) != blk(ci, s // 2)), lmat, 0.0)
            et = _mm(e, tinv, p, p, _BNN)
            yield
            tinv = tinv - _mm(tinv, et, p, p, _BNN)
            yield
            s *= 2
        sol = _mm(tinv, jnp.concatenate([v * beta, kb * eg], axis=2), p, p, _BNN)
        yield
        u_scr[par] = sol[:, :, :dk]
        wq_scr[par] = jnp.concatenate([sol[:, :, dk:], q * eg], axis=1)
        a_scr[par] = amat
        kd_scr[par] = k * jnp.exp(glast - gc)
        cd_scr[par] = jnp.broadcast_to(jnp.exp(glast), cd_scr.shape[1:])

    def recur():
        ya_old, zs_old = ya_scr[par], zs_scr[par]
        s = jnp.where(first_c, 0.0, s_scr[...])
        for c in range(nchunk):
            sl = pl.ds(c * nh, nh)
            r = _mm(wq_scr[prev, sl], s, p, p, _BNN)
            yield
            v_new = u_scr[prev, sl] - r[:, :c_len]
            o_scr[c] = r[:, c_len:] + _mm(a_scr[prev, sl], v_new, p, p, _BNN)
            s = s * cd_scr[prev, sl] + _mm(kd_scr[prev, sl], v_new, p, p, _BTN)
            yield
        s_scr[...] = s
        m = modc_ref[0]
        ndw = ndw_ref[...]
        parts = [ya_old]
        for hh in range(nh):
            seg = o_scr[:, hh].reshape(tm, dk)
            ms = jnp.mean(seg * seg, axis=-1, keepdims=True)
            zs = zs_old[:, hh * dk:(hh + 1) * dk].astype(F32)
            parts.append((seg * lax.rsqrt(ms + EPS) * ndw * zs).astype(BF16))
        yield
        y = _mm(jnp.concatenate(parts, axis=1), wout_ref[...])
        yield
        y_ref[0] = xc_ref[0] + m[2:3] * y

    _run_interleaved([recur(), prepare(), project()], [2, 2, 1])


def _mix(x, mod_i, norm1_w, win_main, w_gates, conv_a_w, norm_a_w, gmat, conv_qkv_w, alog_pad, dtb_pad,
         norm_dn_w, wout, *, tm, passes):
    b, l, d = x.shape
    aw = conv_a_w.shape[-1]
    dnw = conv_qkv_w.shape[-1] // 3
    pm = win_main.shape[-1]
    nt = l // tm
    ntiles = b * nt
    nchunk = tm // CHUNK
    n = nchunk * DN_HEADS
    dk = DN_HEAD_DIM

    def tile(g, lag):
        t = jnp.clip(g - lag, 0, ntiles - 1)
        return t // nt, t % nt

    x_spec = lambda lag: pl.BlockSpec((1, tm, d), lambda g: (*tile(g, lag), 0))
    mod_spec = lambda lag: pl.BlockSpec((1, N_MOD, d), lambda g: (tile(g, lag)[0], 0, 0))
    fix2 = lambda g: (0, 0)
    const = lambda a, **kw: pl.BlockSpec(a.shape, fix2, **kw)
    chunked = (nchunk, DN_HEADS, CHUNK, dk)
    scratch = [
        pltpu.VMEM((tm, d), BF16),
        pltpu.VMEM((HALO, aw), F32),
        pltpu.VMEM((3, HALO, dnw), F32),
        pltpu.VMEM((2,) + chunked, F32),
        pltpu.VMEM((2,) + chunked, F32),
        pltpu.VMEM((2,) + chunked, F32),
        pltpu.VMEM((2, tm, LANES), F32),
        pltpu.VMEM((2, tm, aw), BF16),
        pltpu.VMEM((2, tm, dnw), BF16),
        pltpu.VMEM((DN_HEADS, dk, dk), F32),
        pltpu.VMEM((2, n, CHUNK, dk), F32),
        pltpu.VMEM((2, n, 2 * CHUNK, dk), F32),
        pltpu.VMEM((2, n, CHUNK, CHUNK), F32),
        pltpu.VMEM((2, n, CHUNK, dk), F32),
        pltpu.VMEM((2, n, 1, dk), F32),
        pltpu.VMEM(chunked, F32),
    ]
    est = (6 * tm * d * 4 + (pm + LANES + d) * d * 2 + aw * aw * 2 + tm * d * 2 + 6 * tm * dnw * 4
           + 2 * tm * LANES * 4 + 2 * tm * (aw + dnw) * 2 + 2 * n * (4 * CHUNK + 2 * CHUNK + HALO) * dk * 4
           + tm * dnw * 4 + 24 * 1024 * 1024)
    kern = functools.partial(_mix_kernel, nchunk=nchunk, nt=nt, ntiles=ntiles, aw=aw, dnw=dnw, p=passes)
    return pl.pallas_call(
        kern,
        grid=(ntiles + 2,),
        in_specs=[
            x_spec(0), mod_spec(0), x_spec(2), mod_spec(2),
            const(norm1_w),
            const(win_main, pipeline_mode=pl.Buffered(1)),
            const(w_gates), const(conv_a_w), const(norm_a_w), const(gmat), const(conv_qkv_w),
            const(alog_pad), const(dtb_pad), const(norm_dn_w),
            const(wout, pipeline_mode=pl.Buffered(1)),
        ],
        out_specs=x_spec(2),
        out_shape=jax.ShapeDtypeStruct((b, l, d), F32),
        scratch_shapes=scratch,
        compiler_params=pltpu.CompilerParams(
            dimension_semantics=("arbitrary",), vmem_limit_bytes=_vmem_limit(est)),
        name="token_mix",
    )(x, mod_i, x, mod_i, norm1_w, win_main, w_gates, conv_a_w, norm_a_w, gmat, conv_qkv_w, alog_pad, dtb_pad,
      norm_dn_w, wout)


def _ffn_kernel(x_ref, mod_ref, nw_ref, wup_ref, cw_ref, wdn_ref, nf_ref, y_ref,
                h_scr, acc_scr, halo_scr, u_scr, act_scr, *, nblk, final):
    t = pl.program_id(1)

    @pl.when(t == 0)
    def _():
        halo_scr[...] = jnp.zeros_like(halo_scr)

    m = mod_ref[0]
    x = x_ref[0]
    h = _norm_mod(x, nw_ref[...], m[4:5], m[3:4])
    h_scr[...] = h.astype(BF16)
    acc_scr[...] = jnp.zeros_like(acc_scr)
    tm = x.shape[0]

    def up(j, slot):
        u_scr[slot] = _mm(h_scr[...], wup_ref[j])

    def down(j, slot):
        w = cw_ref[j]
        half = j % 2
        for r in range(0, tm, STRIP):
            prev = halo_scr[j] if r == 0 else u_scr[slot, r - HALO:r]
            ext = jnp.concatenate([prev, u_scr[slot, r:r + STRIP]], axis=0)
            cv = (ext[HALO:] * w[2:3] + ext[HALO - 1:HALO - 1 + STRIP] * w[1:2]
                  + ext[HALO - 2:HALO - 2 + STRIP] * w[0:1])
            act = (_silu(cv[:, :FF_BLK]) * cv[:, FF_BLK:]).astype(BF16)
            act_scr[r:r + STRIP, half * FF_BLK:(half + 1) * FF_BLK] = act
        halo_scr[j] = u_scr[slot, tm - HALO:tm]
        if half == 1:
            acc_scr[...] += _mm(act_scr[...], wdn_ref[j - 1:j + 1].reshape(2 * FF_BLK, -1))
        elif j == nblk - 1:
            acc_scr[...] += _mm(act_scr[:, :FF_BLK], wdn_ref[j])

    up(0, 0)
    for j in range(nblk):
        if j + 1 < nblk:
            up(j + 1, (j + 1) % 2)
        down(j, j % 2)
    y = x + m[5:6] * acc_scr[...]
    if final:
        ms = jnp.mean(y * y, axis=-1, keepdims=True)
        y = y * lax.rsqrt(ms + EPS) * nf_ref[...]
    y_ref[0] = y


def _ffn(x, mod_i, norm2_w, wup_blk, cw_blk, wdn_blk, norm_f_w, *, tm, final):
    b, l, d = x.shape
    nblk = wup_blk.shape[0]
    nt = l // tm
    row = lambda bi, ti: (bi, ti, 0)
    fix2 = lambda bi, ti: (0, 0)
    fix3 = lambda bi, ti: (0, 0, 0)
    est = (4 * tm * d * 4 + wup_blk.size * 2 + wdn_blk.size * 2 + tm * d * 2 + tm * d * 4
           + nblk * HALO * 2 * FF_BLK * 4 + 10 * tm * 2 * FF_BLK * 4 + 4 * 1024 * 1024)
    kern = functools.partial(_ffn_kernel, nblk=nblk, final=final)
    return pl.pallas_call(
        kern,
        grid=(b, nt),
        in_specs=[
            pl.BlockSpec((1, tm, d), row),
            pl.BlockSpec((1, N_MOD, d), lambda bi, ti: (bi, 0, 0)),
            pl.BlockSpec((1, d), fix2),
            pl.BlockSpec(wup_blk.shape, fix3, pipeline_mode=pl.Buffered(1)),
            pl.BlockSpec(cw_blk.shape, fix3),
            pl.BlockSpec(wdn_blk.shape, fix3, pipeline_mode=pl.Buffered(1)),
            pl.BlockSpec((1, d), fix2),
        ],
        out_specs=pl.BlockSpec((1, tm, d), row),
        out_shape=jax.ShapeDtypeStruct((b, l, d), F32),
        scratch_shapes=[
            pltpu.VMEM((tm, d), BF16),
            pltpu.VMEM((tm, d), F32),
            pltpu.VMEM((nblk, HALO, 2 * FF_BLK), F32),
            pltpu.VMEM((2, tm, 2 * FF_BLK), F32),
            pltpu.VMEM((tm, 2 * FF_BLK), BF16),
        ],
        compiler_params=pltpu.CompilerParams(
            dimension_semantics=("arbitrary", "arbitrary"), vmem_limit_bytes=_vmem_limit(est)),
        name="ffn",
    )(x, mod_i, norm2_w, wup_blk, cw_blk, wdn_blk, norm_f_w)


def _pad_rows(w, rows):
    return jnp.pad(w, ((0, rows - w.shape[0]), (0, 0)))


def kernel(x, c, ada_w, ada_b, norm1_w, w_in, conv_a_w, norm_a_w, conv_qkv_w, a_log, dt_bias, norm_dn_w,
           w_out, norm2_w, w_up, conv_ff_w, w_down, norm_f_w):
    b, l, d = x.shape
    depth = ada_w.shape[0]
    aw = conv_a_w.shape[-1]
    dnw = conv_qkv_w.shape[-1] // 3
    d_ff = w_down.shape[1]
    pm = 3 * aw + 4 * dnw
    nblk = d_ff // FF_BLK
    assert d_ff % FF_BLK == 0 and dnw == DN_HEADS * DN_HEAD_DIM and w_in.shape[-1] == pm + 2 * DN_HEADS
    tm = min(512, l)
    tm_ffn = min(256, l)
    assert l % tm == 0 and tm % CHUNK == 0 and l % tm_ffn == 0 and tm_ffn % STRIP == 0

    mod = _modulation(c, ada_w, ada_b)

    gid = jnp.arange(aw) // A_GROUP_DIM
    gmat = (gid[:, None] == gid[None, :]).astype(BF16)

    for i in range(depth):
        win_main = w_in[i, :, :pm].astype(BF16)
        w_gates = jnp.pad(w_in[i, :, pm:], ((0, 0), (0, LANES - 2 * DN_HEADS))).astype(BF16)
        alog_pad = jnp.pad(a_log[i][None, :], ((0, 0), (DN_HEADS, LANES - 2 * DN_HEADS)))
        dtb_pad = jnp.pad(dt_bias[i][None, :], ((0, 0), (DN_HEADS, LANES - 2 * DN_HEADS)))
        x = _mix(x, mod[i], norm1_w[i][None, :], win_main, w_gates, _pad_rows(conv_a_w[i], HALO),
                 norm_a_w[i][None, :], gmat, _pad_rows(conv_qkv_w[i], HALO), alog_pad, dtb_pad,
                 norm_dn_w[i][None, :], w_out[i].astype(BF16), tm=tm, passes=1)

        wu = w_up[i]
        wup_blk = jnp.concatenate(
            [wu[:, :d_ff].reshape(d, nblk, FF_BLK), wu[:, d_ff:].reshape(d, nblk, FF_BLK)], axis=-1)
        wup_blk = jnp.transpose(wup_blk, (1, 0, 2)).astype(BF16)
        cw = conv_ff_w[i]
        cw_blk = jnp.concatenate(
            [cw[:, :d_ff].reshape(-1, nblk, FF_BLK), cw[:, d_ff:].reshape(-1, nblk, FF_BLK)], axis=-1)
        cw_blk = jnp.pad(jnp.transpose(cw_blk, (1, 0, 2)), ((0, 0), (0, HALO - cw.shape[0]), (0, 0)))
        wdn_blk = w_down[i].reshape(nblk, FF_BLK, d).astype(BF16)
        x = _ffn(x, mod[i], norm2_w[i][None, :], wup_blk, cw_blk, wdn_blk, norm_f_w[None, :],
                 tm=tm_ffn, final=(i == depth - 1))
    return x
```
